```python
import math
import jax, jax.numpy as jnp
from jax import lax
import numpy as np

D_MODEL = 1024
BATCH = 2
SEQ = 8192
DEPTH = 2

N_MIXERS = 2
N_CONV_LAYERS = (DEPTH + N_MIXERS - 1) // N_MIXERS
N_ATTN_LAYERS = DEPTH // N_MIXERS
CONV_WIDTH = 3
N_HEADS = 16
N_KV_HEADS = 4
HEAD_DIM = D_MODEL // N_HEADS
GROUP = N_HEADS // N_KV_HEADS
WINDOW = 128
BLOCK = 128
NEG_INF = -1e30
N_BUCKETS = 32
MAX_DISTANCE = 128
PEER_HEADS = 8
N_KEYS = 128
N_EXPERTS = N_KEYS * N_KEYS
PEER_TOPK = 16
QUERY_DIM = 256
SUB_DIM = QUERY_DIM // 2
PEER_CHUNK = 128
RMS_EPS = 1e-6

kernel_name = "hybrid_conv_swa_peer_encoder"


def _rmsnorm(x, g):
    xf = x.astype(jnp.float32)
    y = xf * lax.rsqrt(jnp.mean(xf * xf, axis=-1, keepdims=True) + RMS_EPS)
    return (y * g.astype(jnp.float32)).astype(x.dtype)


def _short_conv_mixer(x, w_in, w_conv, w_out):
    gate_b, gate_c, h = jnp.split(x @ w_in, 3, axis=-1)
    y = lax.conv_general_dilated(
        gate_c * h, w_conv[:, None, :].astype(h.dtype),
        window_strides=(1,), padding=((CONV_WIDTH // 2, CONV_WIDTH // 2),),
        dimension_numbers=("NWC", "WIO", "NWC"), feature_group_count=D_MODEL)
    return (gate_b * y) @ w_out


def _t5_bucket(rel):
    half = N_BUCKETS // 2
    max_exact = half // 2
    ret = jnp.where(rel > 0, half, 0)
    n = jnp.abs(rel)
    nf = jnp.maximum(n, 1).astype(jnp.float32)
    large = max_exact + (jnp.log(nf / max_exact) / math.log(MAX_DISTANCE / max_exact)
                         * (half - max_exact)).astype(jnp.int32)
    large = jnp.minimum(large, half - 1)
    return ret + jnp.where(n < max_exact, n, large)


def _windowed_gqa(x, w_qkv, sink, w_o, rel_bias):
    bsz, s, _ = x.shape
    nb = s // BLOCK
    qkv = x @ w_qkv
    q = qkv[..., :N_HEADS * HEAD_DIM].reshape(bsz, nb, BLOCK, N_KV_HEADS, GROUP, HEAD_DIM)
    k = qkv[..., N_HEADS * HEAD_DIM:(N_HEADS + N_KV_HEADS) * HEAD_DIM].reshape(bsz, s, N_KV_HEADS, HEAD_DIM)
    v = qkv[..., (N_HEADS + N_KV_HEADS) * HEAD_DIM:].reshape(bsz, s, N_KV_HEADS, HEAD_DIM)

    def band(t):
        tp = jnp.pad(t, ((0, 0), (BLOCK, BLOCK), (0, 0), (0, 0)))
        tb = tp.reshape(bsz, nb + 2, BLOCK, N_KV_HEADS, HEAD_DIM)
        return jnp.concatenate([tb[:, :-2], tb[:, 1:-1], tb[:, 2:]], axis=2)

    kw, vw = band(k), band(v)
    scores = jnp.einsum("bnqhgd,bnkhd->bnhgqk", q, kw).astype(jnp.float32) / math.sqrt(HEAD_DIM)

    qi = jnp.arange(BLOCK)[:, None]
    kj = jnp.arange(3 * BLOCK)[None, :]
    rel = kj - BLOCK - qi
    bias = rel_bias[_t5_bucket(rel)].astype(jnp.float32)
    bias = jnp.transpose(bias, (2, 0, 1)).reshape(N_KV_HEADS, GROUP, BLOCK, 3 * BLOCK)
    kpos = jnp.arange(nb)[:, None] * BLOCK - BLOCK + jnp.arange(3 * BLOCK)[None, :]
    valid = (jnp.abs(rel) <= WINDOW)[None] & ((kpos >= 0) & (kpos < s))[:, None, :]
    logits = jnp.where(valid[None, :, None, None], scores + bias, NEG_INF)

    sink_logit = jnp.broadcast_to(sink.astype(jnp.float32).reshape(N_KV_HEADS, GROUP, 1, 1),
                                  logits.shape[:-1] + (1,))
    probs = jax.nn.softmax(jnp.concatenate([logits, sink_logit], axis=-1), axis=-1)[..., :-1]
    out = jnp.einsum("bnhgqk,bnkhd->bnqhgd", probs.astype(vw.dtype), vw)
    return out.reshape(bsz, s, N_HEADS * HEAD_DIM) @ w_o


def _peer(xn, w_q, subkeys, u_tab, v_tab):
    bsz, s, d = xn.shape
    t = xn.reshape(-1, d)
    n_tok = t.shape[0]
    q = (t @ w_q).reshape(n_tok, PEER_HEADS, 2, SUB_DIM)
    sc = jnp.einsum("thpd,hpnd->thpn", q, subkeys).astype(jnp.float32)
    sv, si = lax.top_k(sc, PEER_TOPK)
    cand = (sv[:, :, 0, :, None] + sv[:, :, 1, None, :]).reshape(n_tok, PEER_HEADS, PEER_TOPK * PEER_TOPK)
    cidx = (si[:, :, 0, :, None] * N_KEYS + si[:, :, 1, None, :]).reshape(n_tok, PEER_HEADS, PEER_TOPK * PEER_TOPK)
    top_v, pos = lax.top_k(cand, PEER_TOPK)
    idx = jnp.take_along_axis(cidx, pos, axis=-1)
    g = jax.nn.softmax(top_v, axis=-1)

    n_chunks = n_tok // PEER_CHUNK
    n_sel = PEER_HEADS * PEER_TOPK

    def expert_block(args):
        xc, ic, gc = args
        u = jnp.take(u_tab, ic, axis=0)
        h = jnp.einsum("cd,ced->ce", xc, u)
        a = gc.astype(xc.dtype) * jax.nn.gelu(h, approximate=False)
        return jnp.einsum("ce,ced->cd", a, jnp.take(v_tab, ic, axis=0))

    out = lax.map(expert_block, (t.reshape(n_chunks, PEER_CHUNK, d),
                                 idx.reshape(n_chunks, PEER_CHUNK, n_sel),
                                 g.reshape(n_chunks, PEER_CHUNK, n_sel)))
    return out.reshape(bsz, s, d)


def setup_inputs(seed: int = 0) -> dict:
    key = jax.random.key(seed)
    ks = jax.random.split(key, 20)
    D = D_MODEL
    nrm = lambda k, shape, scale: jax.random.normal(k, shape, jnp.float32) * scale
    qkv_cols = (N_HEADS + 2 * N_KV_HEADS) * HEAD_DIM
    return {
        "x": nrm(ks[0], (BATCH, SEQ, D), 1.0),
        "conv_norm_g": 1.0 + nrm(ks[1], (N_CONV_LAYERS, D), 0.02),
        "conv_w_in": nrm(ks[2], (N_CONV_LAYERS, D, 3 * D), D ** -0.5),
        "conv_w": nrm(ks[3], (N_CONV_LAYERS, CONV_WIDTH, D), CONV_WIDTH ** -0.5),
        "conv_w_out": nrm(ks[4], (N_CONV_LAYERS, D, D), D ** -0.5),
        "attn_norm_g": 1.0 + nrm(ks[5], (N_ATTN_LAYERS, D), 0.02),
        "attn_w_qkv": nrm(ks[6], (N_ATTN_LAYERS, D, qkv_cols), D ** -0.5),
        "attn_sink": nrm(ks[7], (N_ATTN_LAYERS, N_HEADS), 0.5),
        "attn_w_o": nrm(ks[8], (N_ATTN_LAYERS, N_HEADS * HEAD_DIM, D), (N_HEADS * HEAD_DIM) ** -0.5),
        "rel_bias": nrm(ks[9], (N_BUCKETS, N_HEADS), 0.1),
        "ffn_norm_g": 1.0 + nrm(ks[10], (DEPTH, D), 0.02),
        "peer_w_q": nrm(ks[11], (DEPTH, D, PEER_HEADS * QUERY_DIM), D ** -0.5),
        "peer_subkeys": nrm(ks[12], (DEPTH, PEER_HEADS, 2, N_KEYS, SUB_DIM), SUB_DIM ** -0.5),
        "peer_u": nrm(ks[13], (DEPTH, N_EXPERTS, D), D ** -0.5),
        "peer_v": nrm(ks[14], (DEPTH, N_EXPERTS, D), D ** -0.5),
        "final_norm_g": 1.0 + nrm(ks[15], (D,), 0.02),
    }


def reference(x, conv_norm_g, conv_w_in, conv_w, conv_w_out, attn_norm_g, attn_w_qkv,
              attn_sink, attn_w_o, rel_bias, ffn_norm_g, peer_w_q, peer_subkeys,
              peer_u, peer_v, final_norm_g):
    for i in range(DEPTH):
        j = i // N_MIXERS
        if i % N_MIXERS == 0:
            x = x + _short_conv_mixer(_rmsnorm(x, conv_norm_g[j]), conv_w_in[j], conv_w[j], conv_w_out[j])
        else:
            x = x + _windowed_gqa(_rmsnorm(x, attn_norm_g[j]), attn_w_qkv[j], attn_sink[j],
                                  attn_w_o[j], rel_bias)
        x = x + _peer(_rmsnorm(x, ffn_norm_g[i]), peer_w_q[i], peer_subkeys[i], peer_u[i], peer_v[i])
    return _rmsnorm(x, final_norm_g)
```

```python
import functools
import math

import jax
import jax.numpy as jnp
import numpy as np
from jax import lax
from jax.experimental import pallas as pl
from jax.experimental.pallas import tpu as pltpu

F32 = jnp.float32
BF16 = jnp.bfloat16

RMS_EPS = 1e-6
CONV_WIDTH = 3
N_HEADS = 16
N_KV_HEADS = 4
HEAD_DIM = 64
GROUP = N_HEADS // N_KV_HEADS
WINDOW = 128
BLOCK = 128
NEG_INF = -1e30
N_BUCKETS = 32
MAX_DISTANCE = 128
PEER_HEADS = 8
N_KEYS = 128
PEER_TOPK = 16
SUB_DIM = 128

SUBLANES = 8
LANES = 128
VMEM_LIMIT_BYTES = 52 * 1024 * 1024

CONV_TILE = 512
QKV_TILE = 512
ROUTE_TILE = 256
DENSE_TILE = 512
EXPERT_CHUNK = 1024


def _params(*semantics):
    return pltpu.CompilerParams(dimension_semantics=semantics, vmem_limit_bytes=VMEM_LIMIT_BYTES)


def _rmsnorm(x, g):
    ms = jnp.mean(x * x, axis=-1, keepdims=True)
    return x * lax.rsqrt(ms + RMS_EPS) * g


def _conv_in_kernel(x_ref, g_ref, w_ref, u_ref, gb_ref):
    d = x_ref.shape[1]
    xn = _rmsnorm(x_ref[...], g_ref[...]).astype(BF16)
    gb_ref[...] = jnp.dot(xn, w_ref[:, :d], preferred_element_type=F32)
    gate_c = jnp.dot(xn, w_ref[:, d:2 * d], preferred_element_type=F32)
    h = jnp.dot(xn, w_ref[:, 2 * d:], preferred_element_type=F32)
    u_ref[...] = gate_c * h


def _conv_in(x2, g, w_in):
    t, d = x2.shape
    tile = min(CONV_TILE, t)
    return pl.pallas_call(
        _conv_in_kernel,
        grid=(t // tile,),
        in_specs=[
            pl.BlockSpec((tile, d), lambda i: (i, 0)),
            pl.BlockSpec((1, d), lambda i: (0, 0)),
            pl.BlockSpec((d, 3 * d), lambda i: (0, 0)),
        ],
        out_specs=[
            pl.BlockSpec((tile, d), lambda i: (i, 0)),
            pl.BlockSpec((tile, d), lambda i: (i, 0)),
        ],
        out_shape=[jax.ShapeDtypeStruct((t, d), F32), jax.ShapeDtypeStruct((t, d), F32)],
        compiler_params=_params("parallel"),
        name="conv_in",
    )(x2, g, w_in)


def _conv_out_kernel(tiles_per_seq, u_ref, uprev_ref, unext_ref, gb_ref, x_ref, cw_ref, w_ref, o_ref):
    tile = u_ref.shape[0]
    pos = pl.program_id(0) % tiles_per_seq
    u = u_ref[...]
    prev_row = jnp.where(pos == 0, 0.0, uprev_ref[SUBLANES - 1:SUBLANES, :])
    next_row = jnp.where(pos == tiles_per_seq - 1, 0.0, unext_ref[0:1, :])
    row = lax.broadcasted_iota(jnp.int32, u.shape, 0)
    u_before = jnp.where(row == 0, prev_row, pltpu.roll(u, 1, axis=0))
    u_after = jnp.where(row == tile - 1, next_row, pltpu.roll(u, tile - 1, axis=0))
    y = cw_ref[0:1, :] * u_before + cw_ref[1:2, :] * u + cw_ref[2:3, :] * u_after
    z = (gb_ref[...] * y).astype(BF16)
    o_ref[...] = x_ref[...] + jnp.dot(z, w_ref[...], preferred_element_type=F32)


def _conv_out(u, gb, x2, conv_w, w_out, seq):
    t, d = x2.shape
    tile = min(CONV_TILE, seq)
    halo_per_tile = tile // SUBLANES
    n_halo = t // SUBLANES
    return pl.pallas_call(
        functools.partial(_conv_out_kernel, seq // tile),
        grid=(t // tile,),
        in_specs=[
            pl.BlockSpec((tile, d), lambda i: (i, 0)),
            pl.BlockSpec((SUBLANES, d), lambda i: (jnp.maximum(i * halo_per_tile - 1, 0), 0)),
            pl.BlockSpec((SUBLANES, d), lambda i: (jnp.minimum((i + 1) * halo_per_tile, n_halo - 1), 0)),
            pl.BlockSpec((tile, d), lambda i: (i, 0)),
            pl.BlockSpec((tile, d), lambda i: (i, 0)),
            pl.BlockSpec((CONV_WIDTH, d), lambda i: (0, 0)),
            pl.BlockSpec((d, d), lambda i: (0, 0)),
        ],
        out_specs=pl.BlockSpec((tile, d), lambda i: (i, 0)),
        out_shape=jax.ShapeDtypeStruct((t, d), F32),
        compiler_params=_params("parallel"),
        name="conv_out",
    )(u, u, u, gb, x2, conv_w, w_out)


def _attn_qkv_kernel(x_ref, g_ref, w_ref, q_ref, k_ref, v_ref):
    nq = q_ref.shape[1]
    nk = k_ref.shape[1]
    xn = _rmsnorm(x_ref[...], g_ref[...]).astype(BF16)
    q = jnp.dot(xn, w_ref[:, :nq], preferred_element_type=F32) * (1.0 / math.sqrt(HEAD_DIM))
    q_ref[...] = q.astype(BF16)
    k_ref[...] = jnp.dot(xn, w_ref[:, nq:nq + nk], preferred_element_type=F32).astype(BF16)
    v_ref[...] = jnp.dot(xn, w_ref[:, nq + nk:], preferred_element_type=F32).astype(BF16)


def _attn_qkv(x2, g, w_qkv):
    t, d = x2.shape
    tile = min(QKV_TILE, t)
    nq = N_HEADS * HEAD_DIM
    nk = N_KV_HEADS * HEAD_DIM
    return pl.pallas_call(
        _attn_qkv_kernel,
        grid=(t // tile,),
        in_specs=[
            pl.BlockSpec((tile, d), lambda i: (i, 0)),
            pl.BlockSpec((1, d), lambda i: (0, 0)),
            pl.BlockSpec((d, nq + 2 * nk), lambda i: (0, 0)),
        ],
        out_specs=[
            pl.BlockSpec((tile, nq), lambda i: (i, 0)),
            pl.BlockSpec((tile, nk), lambda i: (i, 0)),
            pl.BlockSpec((tile, nk), lambda i: (i, 0)),
        ],
        out_shape=[
            jax.ShapeDtypeStruct((t, nq), BF16),
            jax.ShapeDtypeStruct((t, nk), BF16),
            jax.ShapeDtypeStruct((t, nk), BF16),
        ],
        compiler_params=_params("parallel"),
        name="attn_qkv",
    )(x2, g, w_qkv)


def _attn_kernel(blocks_per_seq, q_ref, kp_ref, kc_ref, kn_ref, vp_ref, vc_ref, vn_ref, bucket_ref,
                 relb_ref, sink_ref, x_ref, wo_ref, o_ref, bias_ref, att_ref):
    step = pl.program_id(0)

    @pl.when(step == 0)
    def _():
        bucket = bucket_ref[...]
        rel = lax.broadcasted_iota(jnp.int32, bucket.shape, 1) - BLOCK - lax.broadcasted_iota(
            jnp.int32, bucket.shape, 0)
        in_window = jnp.abs(rel) <= WINDOW

        def head_body(hq, carry):
            def bucket_body(b, acc):
                return jnp.where(bucket == b, relb_ref[b, hq], acc)

            table = lax.fori_loop(0, N_BUCKETS, bucket_body, jnp.zeros(bucket.shape, F32))
            bias_ref[hq] = jnp.where(in_window, table, NEG_INF)
            return carry

        lax.fori_loop(0, N_HEADS, head_body, 0)

    pos = step % blocks_per_seq
    has_prev = pos > 0
    has_next = pos < blocks_per_seq - 1
    col = lax.broadcasted_iota(jnp.int32, (BLOCK, 3 * BLOCK), 1)
    key_ok = ((col >= BLOCK) | has_prev) & ((col < 2 * BLOCK) | has_next)

    for kvh in range(N_KV_HEADS):
        lo = kvh * HEAD_DIM
        k = jnp.concatenate([kp_ref[:, lo:lo + HEAD_DIM], kc_ref[:, lo:lo + HEAD_DIM],
                             kn_ref[:, lo:lo + HEAD_DIM]], axis=0)
        v = jnp.concatenate([vp_ref[:, lo:lo + HEAD_DIM], vc_ref[:, lo:lo + HEAD_DIM],
                             vn_ref[:, lo:lo + HEAD_DIM]], axis=0)
        for g in range(GROUP):
            hq = kvh * GROUP + g
            q = q_ref[:, hq * HEAD_DIM:(hq + 1) * HEAD_DIM]
            s = lax.dot_general(q, k, (((1,), (1,)), ((), ())), preferred_element_type=F32)
            logits = jnp.where(key_ok, s + bias_ref[hq], NEG_INF)
            sink = sink_ref[hq]
            m = jnp.maximum(jnp.max(logits, axis=-1, keepdims=True), sink)
            p = jnp.exp(logits - m)
            den = jnp.sum(p, axis=-1, keepdims=True) + jnp.exp(sink - m)
            o = jnp.dot(p.astype(BF16), v, preferred_element_type=F32) / den
            att_ref[:, hq * HEAD_DIM:(hq + 1) * HEAD_DIM] = o.astype(BF16)

    o_ref[...] = x_ref[...] + jnp.dot(att_ref[...], wo_ref[...], preferred_element_type=F32)


def _attn(q, k, v, bucket, rel_bias, sink, x2, w_o, seq):
    t, d = x2.shape
    nq = N_HEADS * HEAD_DIM
    nk = N_KV_HEADS * HEAD_DIM
    n_blocks = t // BLOCK
    prev_map = lambda i: (jnp.maximum(i - 1, 0), 0)
    cur_map = lambda i: (i, 0)
    next_map = lambda i: (jnp.minimum(i + 1, n_blocks - 1), 0)
    smem = pl.BlockSpec(memory_space=pltpu.SMEM)
    return pl.pallas_call(
        functools.partial(_attn_kernel, seq // BLOCK),
        grid=(n_blocks,),
        in_specs=[
            pl.BlockSpec((BLOCK, nq), cur_map),
            pl.BlockSpec((BLOCK, nk), prev_map),
            pl.BlockSpec((BLOCK, nk), cur_map),
            pl.BlockSpec((BLOCK, nk), next_map),
            pl.BlockSpec((BLOCK, nk), prev_map),
            pl.BlockSpec((BLOCK, nk), cur_map),
            pl.BlockSpec((BLOCK, nk), next_map),
            pl.BlockSpec((BLOCK, 3 * BLOCK), lambda i: (0, 0)),
            smem,
            smem,
            pl.BlockSpec((BLOCK, d), cur_map),
            pl.BlockSpec((nq, d), lambda i: (0, 0)),
        ],
        out_specs=pl.BlockSpec((BLOCK, d), cur_map),
        out_shape=jax.ShapeDtypeStruct((t, d), F32),
        scratch_shapes=[
            pltpu.VMEM((N_HEADS, BLOCK, 3 * BLOCK), F32),
            pltpu.VMEM((BLOCK, nq), BF16),
        ],
        compiler_params=_params("arbitrary"),
        name="attn",
    )(q, k, k, k, v, v, v, bucket, rel_bias, sink, x2, w_o)


def _t5_bucket_table():
    half = N_BUCKETS // 2
    max_exact = half // 2
    qi = jnp.arange(BLOCK)[:, None]
    kj = jnp.arange(3 * BLOCK)[None, :]
    rel = kj - BLOCK - qi
    ret = jnp.where(rel > 0, half, 0)
    n = jnp.abs(rel)
    nf = jnp.maximum(n, 1).astype(jnp.float32)
    large = max_exact + (jnp.log(nf / max_exact) / math.log(MAX_DISTANCE / max_exact)
                         * (half - max_exact)).astype(jnp.int32)
    large = jnp.minimum(large, half - 1)
    return (ret + jnp.where(n < max_exact, n, large)).astype(jnp.int32)


def _top16_ranks(s, key_iota):
    rank = jnp.full(s.shape, float(PEER_TOPK), F32)
    slot = lax.broadcasted_iota(jnp.int32, (PEER_TOPK, s.shape[1]), 0)
    values = jnp.zeros((PEER_TOPK, s.shape[1]), F32)
    for r in range(PEER_TOPK):
        m = jnp.max(s, axis=0, keepdims=True)
        first = jnp.min(jnp.where(s == m, key_iota, N_KEYS), axis=0, keepdims=True)
        taken = key_iota == first
        s = jnp.where(taken, -jnp.inf, s)
        rank = jnp.where(taken, float(r), rank)
        values = jnp.where(slot == r, m, values)
    return rank, values


def _staircase(sv1, sv2):
    lanes = sv1.shape[1]
    slot = lax.broadcasted_iota(jnp.int32, (PEER_TOPK, lanes), 0)
    n = jnp.zeros((PEER_TOPK, lanes), jnp.int32)
    front = sv1[0:1, :] + sv2
    big = PEER_TOPK * PEER_TOPK + PEER_TOPK
    for r in range(PEER_TOPK):
        m = jnp.max(front, axis=0, keepdims=True)
        flat = n * PEER_TOPK + slot
        first = jnp.min(jnp.where(front == m, flat, big), axis=0, keepdims=True)
        chosen = flat == first
        n = n + chosen.astype(jnp.int32)
        if r + 1 < PEER_TOPK:
            depth = jnp.sum(jnp.where(chosen, n, 0), axis=0, keepdims=True)
            nxt = jnp.sum(jnp.where(slot == depth, sv1, 0.0), axis=0, keepdims=True)
            front = jnp.where(chosen, jnp.where(depth < PEER_TOPK, nxt + sv2, -jnp.inf), front)
    return n.astype(F32)


def _peer_route_kernel(x_ref, g_ref, wq_ref, keys_ref, xn_ref, r1_ref, c_ref, n2_ref, w_ref, s_ref):
    tile = x_ref.shape[0]
    xn = _rmsnorm(x_ref[...], g_ref[...]).astype(BF16)
    xn_ref[...] = xn
    q_t = lax.dot_general(wq_ref[...], xn, (((1,), (1,)), ((), ())), preferred_element_type=F32)
    for hc in range(PEER_HEADS * 2):
        q_hc = q_t[hc * SUB_DIM:(hc + 1) * SUB_DIM, :].astype(BF16)
        s_ref[hc] = jnp.dot(keys_ref[hc], q_hc, preferred_element_type=F32)

    key_iota = lax.broadcasted_iota(jnp.int32, (N_KEYS, LANES), 0)
    slot = lax.broadcasted_iota(jnp.int32, (PEER_TOPK, LANES), 0).astype(F32)

    def body(it, carry):
        hd = it // (tile // LANES)
        lane0 = pl.multiple_of((it % (tile // LANES)) * LANES, LANES)
        lanes = pl.ds(lane0, LANES)
        s1 = s_ref[2 * hd, :, lanes]
        s2 = s_ref[2 * hd + 1, :, lanes]
        rank1, sv1 = _top16_ranks(s1, key_iota)
        rank2, sv2 = _top16_ranks(s2, key_iota)
        n = _staircase(sv1, sv2)
        e1 = jnp.exp(s1 - sv1[0:1, :])
        e2 = jnp.exp(s2 - sv2[0:1, :])
        e1s = jnp.exp(sv1 - sv1[0:1, :])
        e2s = jnp.exp(sv2 - sv2[0:1, :])
        inner = jnp.zeros_like(e2s)
        for p in range(PEER_TOPK):
            inner = inner + jnp.where(n > float(p), e1s[p:p + 1, :], 0.0)
        z = jnp.sum(inner * e2s, axis=0, keepdims=True)
        n2 = jnp.zeros_like(rank2)
        for q in range(PEER_TOPK):
            n2 = jnp.where(rank2 == float(q), n[q:q + 1, :], n2)
        r1_ref[hd, :, lanes] = rank1
        c_ref[hd, :, lanes] = e1 / z
        n2_ref[hd, :, lanes] = n2
        w_ref[hd, :, lanes] = e2
        return carry

    lax.fori_loop(0, PEER_HEADS * (tile // LANES), body, 0)


def _peer_route(x2, g, wq_t, keys):
    t, d = x2.shape
    tile = min(ROUTE_TILE, t)
    nqd = PEER_HEADS * 2 * SUB_DIM
    route_shape = jax.ShapeDtypeStruct((PEER_HEADS, N_KEYS, t), F32)
    route_spec = pl.BlockSpec((PEER_HEADS, N_KEYS, tile), lambda i: (0, 0, i))
    return pl.pallas_call(
        _peer_route_kernel,
        grid=(t // tile,),
        in_specs=[
            pl.BlockSpec((tile, d), lambda i: (i, 0)),
            pl.BlockSpec((1, d), lambda i: (0, 0)),
            pl.BlockSpec((nqd, d), lambda i: (0, 0)),
            pl.BlockSpec((PEER_HEADS * 2, N_KEYS, SUB_DIM), lambda i: (0, 0, 0)),
        ],
        out_specs=[pl.BlockSpec((tile, d), lambda i: (i, 0)), route_spec, route_spec, route_spec, route_spec],
        out_shape=[jax.ShapeDtypeStruct((t, d), BF16), route_shape, route_shape, route_shape, route_shape],
        scratch_shapes=[pltpu.VMEM((PEER_HEADS * 2, N_KEYS, tile), F32)],
        compiler_params=_params("parallel"),
        name="peer_route",
    )(x2, g, wq_t, keys)


def _peer_dense_kernel(final_norm, xn_ref, x_ref, u_ref, vt_ref, r1_ref, c_ref, n2_ref, w_ref, fg_ref,
                       o_ref, acc_ref, h_ref, a_ref):
    chunk = pl.program_id(1)
    tile = xn_ref.shape[0]
    rows_per_chunk = u_ref.shape[0] // N_KEYS

    @pl.when(chunk == 0)
    def _():
        acc_ref[...] = jnp.zeros_like(acc_ref)

    h_ref[...] = lax.dot_general(u_ref[...], xn_ref[...], (((1,), (1,)), ((), ())),
                                 preferred_element_type=F32)

    def lane_body(lc, carry):
        lanes = pl.ds(pl.multiple_of(lc * LANES, LANES), LANES)
        for j in range(rows_per_chunk):
            rows = slice(j * N_KEYS, (j + 1) * N_KEYS)
            h = h_ref[rows, lanes]
            act = 0.5 * h * (1.0 + lax.erf(h * (1.0 / math.sqrt(2.0))))
            gate = jnp.zeros((N_KEYS, LANES), F32)
            for hd in range(PEER_HEADS):
                r1 = r1_ref[hd, j:j + 1, lanes]
                c = c_ref[hd, j:j + 1, lanes]
                gate = gate + jnp.where(r1 < n2_ref[hd, :, lanes], w_ref[hd, :, lanes], 0.0) * c
            a_ref[rows, lanes] = (gate * act).astype(BF16)
        return carry

    lax.fori_loop(0, tile // LANES, lane_body, 0)
    acc_ref[...] += jnp.dot(vt_ref[...], a_ref[...], preferred_element_type=F32)

    @pl.when(chunk == pl.num_programs(1) - 1)
    def _():
        y = x_ref[...] + acc_ref[...].T
        if final_norm:
            y = _rmsnorm(y, fg_ref[...])
        o_ref[...] = y


def _peer_dense(xn, x2, u, vt, r1, c, n2, w, final_g, final_norm):
    t, d = x2.shape
    n_experts = u.shape[0]
    tile = min(DENSE_TILE, t)
    ec = EXPERT_CHUNK
    rows = ec // N_KEYS
    return pl.pallas_call(
        functools.partial(_peer_dense_kernel, final_norm),
        grid=(t // tile, n_experts // ec),
        in_specs=[
            pl.BlockSpec((tile, d), lambda i, e: (i, 0)),
            pl.BlockSpec((tile, d), lambda i, e: (i, 0)),
            pl.BlockSpec((ec, d), lambda i, e: (e, 0)),
            pl.BlockSpec((d, ec), lambda i, e: (0, e)),
            pl.BlockSpec((PEER_HEADS, rows, tile), lambda i, e: (0, e, i)),
            pl.BlockSpec((PEER_HEADS, rows, tile), lambda i, e: (0, e, i)),
            pl.BlockSpec((PEER_HEADS, N_KEYS, tile), lambda i, e: (0, 0, i)),
            pl.BlockSpec((PEER_HEADS, N_KEYS, tile), lambda i, e: (0, 0, i)),
            pl.BlockSpec((1, d), lambda i, e: (0, 0)),
        ],
        out_specs=pl.BlockSpec((tile, d), lambda i, e: (i, 0)),
        out_shape=jax.ShapeDtypeStruct((t, d), F32),
        scratch_shapes=[
            pltpu.VMEM((d, tile), F32),
            pltpu.VMEM((ec, tile), F32),
            pltpu.VMEM((ec, tile), BF16),
        ],
        compiler_params=_params("parallel", "arbitrary"),
        name="peer_dense",
    )(xn, x2, u, vt, r1, c, n2, w, final_g)


def _peer(x2, g, w_q, subkeys, u_tab, v_tab, final_g, final_norm):
    d = x2.shape[1]
    wq_t = w_q.T.astype(BF16)
    keys = subkeys.reshape(PEER_HEADS * 2, N_KEYS, SUB_DIM).astype(BF16)
    xn, r1, c, n2, w = _peer_route(x2, g.reshape(1, d), wq_t, keys)
    return _peer_dense(xn, x2, u_tab.astype(BF16), v_tab.T.astype(BF16), r1, c, n2, w,
                       final_g.reshape(1, d), final_norm)


def kernel(x, conv_norm_g, conv_w_in, conv_w, conv_w_out, attn_norm_g, attn_w_qkv, attn_sink, attn_w_o,
           rel_bias, ffn_norm_g, peer_w_q, peer_subkeys, peer_u, peer_v, final_norm_g):
    bsz, seq, d = x.shape
    depth = ffn_norm_g.shape[0]
    x2 = x.reshape(bsz * seq, d)
    bucket = _t5_bucket_table()
    for i in range(depth):
        j = i // 2
        if i % 2 == 0:
            u, gb = _conv_in(x2, conv_norm_g[j].reshape(1, d), conv_w_in[j].astype(BF16))
            x2 = _conv_out(u, gb, x2, conv_w[j], conv_w_out[j].astype(BF16), seq)
        else:
            q, k, v = _attn_qkv(x2, attn_norm_g[j].reshape(1, d), attn_w_qkv[j].astype(BF16))
            x2 = _attn(q, k, v, bucket, rel_bias, attn_sink[j], x2, attn_w_o[j].astype(BF16), seq)
        x2 = _peer(x2, ffn_norm_g[i], peer_w_q[i], peer_subkeys[i], peer_u[i], peer_v[i],
                   final_norm_g, i == depth - 1)
    return x2.reshape(bsz, seq, d)
```

```python
import functools
import math

import jax
import jax.numpy as jnp
import numpy as np
from jax import lax
from jax.experimental import pallas as pl
from jax.experimental.pallas import tpu as pltpu

F32 = jnp.float32
BF16 = jnp.bfloat16
GATE_DTYPE = jnp.bfloat16

RMS_EPS = 1e-6
CONV_WIDTH = 3
N_HEADS = 16
N_KV_HEADS = 4
HEAD_DIM = 64
GROUP = N_HEADS // N_KV_HEADS
WINDOW = 128
BLOCK = 128
NEG_INF = -1e30
N_BUCKETS = 32
MAX_DISTANCE = 128
PEER_HEADS = 8
N_KEYS = 128
PEER_TOPK = 16
SUB_DIM = 128

SUBLANES = 8
LANES = 128
BF16_ROWS = 2 * SUBLANES
VMEM_LIMIT_BYTES = 52 * 1024 * 1024

CONV_TILE = 512
QKV_TILE = 512
ROUTE_TILE = 256
DENSE_TILE = 512
EXPERT_CHUNK = 1024
DENSE_PIECES = 4


def _params(*semantics, flags=None):
    return pltpu.CompilerParams(dimension_semantics=semantics, vmem_limit_bytes=VMEM_LIMIT_BYTES,
                                flags=flags)


def _rmsnorm(x, g):
    ms = jnp.mean(x * x, axis=-1, keepdims=True)
    return x * lax.rsqrt(ms + RMS_EPS) * g


def _conv_in_kernel(x_ref, g_ref, w_ref, u_ref, gb_ref):
    d = x_ref.shape[1]
    xn = _rmsnorm(x_ref[...], g_ref[...]).astype(BF16)
    gb_ref[...] = jnp.dot(xn, w_ref[:, :d], preferred_element_type=F32)
    gate_c = jnp.dot(xn, w_ref[:, d:2 * d], preferred_element_type=F32)
    h = jnp.dot(xn, w_ref[:, 2 * d:], preferred_element_type=F32)
    u_ref[...] = gate_c * h


def _conv_in(x2, g, w_in):
    t, d = x2.shape
    tile = min(CONV_TILE, t)
    return pl.pallas_call(
        _conv_in_kernel,
        grid=(t // tile,),
        in_specs=[
            pl.BlockSpec((tile, d), lambda i: (i, 0)),
            pl.BlockSpec((1, d), lambda i: (0, 0)),
            pl.BlockSpec((d, 3 * d), lambda i: (0, 0)),
        ],
        out_specs=[
            pl.BlockSpec((tile, d), lambda i: (i, 0)),
            pl.BlockSpec((tile, d), lambda i: (i, 0)),
        ],
        out_shape=[jax.ShapeDtypeStruct((t, d), F32), jax.ShapeDtypeStruct((t, d), F32)],
        compiler_params=_params("parallel"),
        name="conv_in",
    )(x2, g, w_in)


def _conv_out_kernel(tiles_per_seq, u_ref, uprev_ref, unext_ref, gb_ref, x_ref, cw_ref, w_ref, o_ref):
    tile = u_ref.shape[0]
    pos = pl.program_id(0) % tiles_per_seq
    u = u_ref[...]
    prev_row = jnp.where(pos == 0, 0.0, uprev_ref[SUBLANES - 1:SUBLANES, :])
    next_row = jnp.where(pos == tiles_per_seq - 1, 0.0, unext_ref[0:1, :])
    row = lax.broadcasted_iota(jnp.int32, u.shape, 0)
    u_before = jnp.where(row == 0, prev_row, pltpu.roll(u, 1, axis=0))
    u_after = jnp.where(row == tile - 1, next_row, pltpu.roll(u, tile - 1, axis=0))
    y = cw_ref[0:1, :] * u_before + cw_ref[1:2, :] * u + cw_ref[2:3, :] * u_after
    z = (gb_ref[...] * y).astype(BF16)
    o_ref[...] = x_ref[...] + jnp.dot(z, w_ref[...], preferred_element_type=F32)


def _conv_out(u, gb, x2, conv_w, w_out, seq):
    t, d = x2.shape
    tile = min(CONV_TILE, seq)
    halo_per_tile = tile // SUBLANES
    n_halo = t // SUBLANES
    return pl.pallas_call(
        functools.partial(_conv_out_kernel, seq // tile),
        grid=(t // tile,),
        in_specs=[
            pl.BlockSpec((tile, d), lambda i: (i, 0)),
            pl.BlockSpec((SUBLANES, d), lambda i: (jnp.maximum(i * halo_per_tile - 1, 0), 0)),
            pl.BlockSpec((SUBLANES, d), lambda i: (jnp.minimum((i + 1) * halo_per_tile, n_halo - 1), 0)),
            pl.BlockSpec((tile, d), lambda i: (i, 0)),
            pl.BlockSpec((tile, d), lambda i: (i, 0)),
            pl.BlockSpec((CONV_WIDTH, d), lambda i: (0, 0)),
            pl.BlockSpec((d, d), lambda i: (0, 0)),
        ],
        out_specs=pl.BlockSpec((tile, d), lambda i: (i, 0)),
        out_shape=jax.ShapeDtypeStruct((t, d), F32),
        compiler_params=_params("parallel"),
        name="conv_out",
    )(u, u, u, gb, x2, conv_w, w_out)


def _attn_qkv_kernel(x_ref, g_ref, w_ref, q_ref, k_ref, v_ref):
    nq = q_ref.shape[1]
    nk = k_ref.shape[1]
    xn = _rmsnorm(x_ref[...], g_ref[...]).astype(BF16)
    q = jnp.dot(xn, w_ref[:, :nq], preferred_element_type=F32) * (1.0 / math.sqrt(HEAD_DIM))
    q_ref[...] = q.astype(BF16)
    k_ref[...] = jnp.dot(xn, w_ref[:, nq:nq + nk], preferred_element_type=F32).astype(BF16)
    v_ref[...] = jnp.dot(xn, w_ref[:, nq + nk:], preferred_element_type=F32).astype(BF16)


def _attn_qkv(x2, g, w_qkv):
    t, d = x2.shape
    tile = min(QKV_TILE, t)
    nq = N_HEADS * HEAD_DIM
    nk = N_KV_HEADS * HEAD_DIM
    return pl.pallas_call(
        _attn_qkv_kernel,
        grid=(t // tile,),
        in_specs=[
            pl.BlockSpec((tile, d), lambda i: (i, 0)),
            pl.BlockSpec((1, d), lambda i: (0, 0)),
            pl.BlockSpec((d, nq + 2 * nk), lambda i: (0, 0)),
        ],
        out_specs=[
            pl.BlockSpec((tile, nq), lambda i: (i, 0)),
            pl.BlockSpec((tile, nk), lambda i: (i, 0)),
            pl.BlockSpec((tile, nk), lambda i: (i, 0)),
        ],
        out_shape=[
            jax.ShapeDtypeStruct((t, nq), BF16),
            jax.ShapeDtypeStruct((t, nk), BF16),
            jax.ShapeDtypeStruct((t, nk), BF16),
        ],
        compiler_params=_params("parallel"),
        name="attn_qkv",
    )(x2, g, w_qkv)


def _attn_kernel(blocks_per_seq, q_ref, kp_ref, kc_ref, kn_ref, vp_ref, vc_ref, vn_ref, bucket_ref,
                 relb_ref, sink_ref, x_ref, wo_ref, o_ref, bias_ref, att_ref):
    step = pl.program_id(0)

    @pl.when(step == 0)
    def _():
        bucket = bucket_ref[...]
        rel = lax.broadcasted_iota(jnp.int32, bucket.shape, 1) - BLOCK - lax.broadcasted_iota(
            jnp.int32, bucket.shape, 0)
        in_window = jnp.abs(rel) <= WINDOW

        def head_body(hq, carry):
            def bucket_body(b, acc):
                return jnp.where(bucket == b, relb_ref[b, hq], acc)

            table = lax.fori_loop(0, N_BUCKETS, bucket_body, jnp.zeros(bucket.shape, F32))
            bias_ref[hq] = jnp.where(in_window, table, NEG_INF)
            return carry

        lax.fori_loop(0, N_HEADS, head_body, 0)

    pos = step % blocks_per_seq
    has_prev = pos > 0
    has_next = pos < blocks_per_seq - 1
    col = lax.broadcasted_iota(jnp.int32, (BLOCK, 3 * BLOCK), 1)
    key_ok = ((col >= BLOCK) | has_prev) & ((col < 2 * BLOCK) | has_next)

    for kvh in range(N_KV_HEADS):
        lo = kvh * HEAD_DIM
        k = jnp.concatenate([kp_ref[:, lo:lo + HEAD_DIM], kc_ref[:, lo:lo + HEAD_DIM],
                             kn_ref[:, lo:lo + HEAD_DIM]], axis=0)
        v = jnp.concatenate([vp_ref[:, lo:lo + HEAD_DIM], vc_ref[:, lo:lo + HEAD_DIM],
                             vn_ref[:, lo:lo + HEAD_DIM]], axis=0)
        for g in range(GROUP):
            hq = kvh * GROUP + g
            q = q_ref[:, hq * HEAD_DIM:(hq + 1) * HEAD_DIM]
            s = lax.dot_general(q, k, (((1,), (1,)), ((), ())), preferred_element_type=F32)
            logits = jnp.where(key_ok, s + bias_ref[hq], NEG_INF)
            sink = sink_ref[hq]
            m = jnp.maximum(jnp.max(logits, axis=-1, keepdims=True), sink)
            p = jnp.exp(logits - m)
            den = jnp.sum(p, axis=-1, keepdims=True) + jnp.exp(sink - m)
            o = jnp.dot(p.astype(BF16), v, preferred_element_type=F32) / den
            att_ref[:, hq * HEAD_DIM:(hq + 1) * HEAD_DIM] = o.astype(BF16)

    o_ref[...] = x_ref[...] + jnp.dot(att_ref[...], wo_ref[...], preferred_element_type=F32)


def _attn(q, k, v, bucket, rel_bias, sink, x2, w_o, seq):
    t, d = x2.shape
    nq = N_HEADS * HEAD_DIM
    nk = N_KV_HEADS * HEAD_DIM
    n_blocks = t // BLOCK
    prev_map = lambda i: (jnp.maximum(i - 1, 0), 0)
    cur_map = lambda i: (i, 0)
    next_map = lambda i: (jnp.minimum(i + 1, n_blocks - 1), 0)
    smem = pl.BlockSpec(memory_space=pltpu.SMEM)
    return pl.pallas_call(
        functools.partial(_attn_kernel, seq // BLOCK),
        grid=(n_blocks,),
        in_specs=[
            pl.BlockSpec((BLOCK, nq), cur_map),
            pl.BlockSpec((BLOCK, nk), prev_map),
            pl.BlockSpec((BLOCK, nk), cur_map),
            pl.BlockSpec((BLOCK, nk), next_map),
            pl.BlockSpec((BLOCK, nk), prev_map),
            pl.BlockSpec((BLOCK, nk), cur_map),
            pl.BlockSpec((BLOCK, nk), next_map),
            pl.BlockSpec((BLOCK, 3 * BLOCK), lambda i: (0, 0)),
            smem,
            smem,
            pl.BlockSpec((BLOCK, d), cur_map),
            pl.BlockSpec((nq, d), lambda i: (0, 0)),
        ],
        out_specs=pl.BlockSpec((BLOCK, d), cur_map),
        out_shape=jax.ShapeDtypeStruct((t, d), F32),
        scratch_shapes=[
            pltpu.VMEM((N_HEADS, BLOCK, 3 * BLOCK), F32),
            pltpu.VMEM((BLOCK, nq), BF16),
        ],
        compiler_params=_params("arbitrary"),
        name="attn",
    )(q, k, k, k, v, v, v, bucket, rel_bias, sink, x2, w_o)


def _t5_bucket_table():
    half = N_BUCKETS // 2
    max_exact = half // 2
    qi = jnp.arange(BLOCK)[:, None]
    kj = jnp.arange(3 * BLOCK)[None, :]
    rel = kj - BLOCK - qi
    ret = jnp.where(rel > 0, half, 0)
    n = jnp.abs(rel)
    nf = jnp.maximum(n, 1).astype(jnp.float32)
    large = max_exact + (jnp.log(nf / max_exact) / math.log(MAX_DISTANCE / max_exact)
                         * (half - max_exact)).astype(jnp.int32)
    large = jnp.minimum(large, half - 1)
    return (ret + jnp.where(n < max_exact, n, large)).astype(jnp.int32)


def _top16_ranks(s, key_iota):
    rank = jnp.full(s.shape, float(PEER_TOPK), F32)
    slot = lax.broadcasted_iota(jnp.int32, (PEER_TOPK, s.shape[1]), 0)
    values = jnp.zeros((PEER_TOPK, s.shape[1]), F32)
    for r in range(PEER_TOPK):
        m = jnp.max(s, axis=0, keepdims=True)
        first = jnp.min(jnp.where(s == m, key_iota, N_KEYS), axis=0, keepdims=True)
        taken = key_iota == first
        s = jnp.where(taken, -jnp.inf, s)
        rank = jnp.where(taken, float(r), rank)
        values = jnp.where(slot == r, m, values)
    return rank, values


def _staircase(sv1, sv2):
    lanes = sv1.shape[1]
    slot = lax.broadcasted_iota(jnp.int32, (PEER_TOPK, lanes), 0)
    n = jnp.zeros((PEER_TOPK, lanes), jnp.int32)
    front = sv1[0:1, :] + sv2
    big = PEER_TOPK * PEER_TOPK + PEER_TOPK
    for r in range(PEER_TOPK):
        m = jnp.max(front, axis=0, keepdims=True)
        flat = n * PEER_TOPK + slot
        first = jnp.min(jnp.where(front == m, flat, big), axis=0, keepdims=True)
        chosen = flat == first
        n = n + chosen.astype(jnp.int32)
        if r + 1 < PEER_TOPK:
            depth = jnp.sum(jnp.where(chosen, n, 0), axis=0, keepdims=True)
            nxt = jnp.sum(jnp.where(slot == depth, sv1, 0.0), axis=0, keepdims=True)
            front = jnp.where(chosen, jnp.where(depth < PEER_TOPK, nxt + sv2, -jnp.inf), front)
    return n.astype(F32)


def _twice_bf16(v):
    bits = lax.bitcast_convert_type(v.astype(GATE_DTYPE).astype(F32), jnp.uint32)
    return lax.bitcast_convert_type(bits | (bits >> 16), jnp.int32)


def _peer_route_kernel(x_ref, g_ref, wq_ref, keys_ref, xn_ref, r1_ref, c_ref, n2_ref, w_ref, s_ref):
    tile = x_ref.shape[0]
    xn_t = _rmsnorm(x_ref[...], g_ref[...]).T.astype(BF16)
    xn_ref[...] = xn_t
    q_t = jnp.dot(wq_ref[...], xn_t, preferred_element_type=F32)
    for hc in range(PEER_HEADS * 2):
        q_hc = q_t[hc * SUB_DIM:(hc + 1) * SUB_DIM, :].astype(BF16)
        s_ref[hc] = jnp.dot(keys_ref[hc], q_hc, preferred_element_type=F32)

    key_iota = lax.broadcasted_iota(jnp.int32, (N_KEYS, LANES), 0)
    slot = lax.broadcasted_iota(jnp.int32, (PEER_TOPK, LANES), 0).astype(F32)

    def body(it, carry):
        hd = it // (tile // LANES)
        lane_chunk = it % (tile // LANES)
        lane0 = pl.multiple_of(lane_chunk * LANES, LANES)
        lanes = pl.ds(lane0, LANES)
        s1 = s_ref[2 * hd, :, lanes]
        s2 = s_ref[2 * hd + 1, :, lanes]
        rank1, sv1 = _top16_ranks(s1, key_iota)
        rank2, sv2 = _top16_ranks(s2, key_iota)
        n = _staircase(sv1, sv2)
        e1 = jnp.exp(s1 - sv1[0:1, :])
        e2 = jnp.exp(s2 - sv2[0:1, :])
        e1s = jnp.exp(sv1 - sv1[0:1, :])
        e2s = jnp.exp(sv2 - sv2[0:1, :])
        inner = jnp.zeros_like(e2s)
        for p in range(PEER_TOPK):
            inner = inner + jnp.where(n > float(p), e1s[p:p + 1, :], 0.0)
        z = jnp.sum(inner * e2s, axis=0, keepdims=True)
        n2 = jnp.zeros_like(rank2)
        for q in range(PEER_TOPK):
            n2 = jnp.where(rank2 == float(q), n[q:q + 1, :], n2)
        r1_ref[hd, lane_chunk] = _twice_bf16(rank1)
        c_ref[hd, lane_chunk] = _twice_bf16(e1 / z)
        n2_ref[hd, :, lanes] = n2.astype(GATE_DTYPE)
        w_ref[hd, :, lanes] = e2.astype(GATE_DTYPE)
        return carry

    lax.fori_loop(0, PEER_HEADS * (tile // LANES), body, 0)


def _peer_route(x2, g, wq_t, keys):
    t, d = x2.shape
    tile = min(ROUTE_TILE, t)
    nqd = PEER_HEADS * 2 * SUB_DIM
    scalar_shape = jax.ShapeDtypeStruct((PEER_HEADS, t // LANES, N_KEYS, LANES), jnp.int32)
    scalar_spec = pl.BlockSpec((PEER_HEADS, tile // LANES, N_KEYS, LANES), lambda i: (0, i, 0, 0))
    table_shape = jax.ShapeDtypeStruct((PEER_HEADS, N_KEYS, t), GATE_DTYPE)
    table_spec = pl.BlockSpec((PEER_HEADS, N_KEYS, tile), lambda i: (0, 0, i))
    return pl.pallas_call(
        _peer_route_kernel,
        grid=(t // tile,),
        in_specs=[
            pl.BlockSpec((tile, d), lambda i: (i, 0)),
            pl.BlockSpec((1, d), lambda i: (0, 0)),
            pl.BlockSpec((nqd, d), lambda i: (0, 0)),
            pl.BlockSpec((PEER_HEADS * 2, N_KEYS, SUB_DIM), lambda i: (0, 0, 0)),
        ],
        out_specs=[pl.BlockSpec((d, tile), lambda i: (0, i)), scalar_spec, scalar_spec, table_spec, table_spec],
        out_shape=[jax.ShapeDtypeStruct((d, t), BF16), scalar_shape, scalar_shape, table_shape, table_shape],
        scratch_shapes=[pltpu.VMEM((PEER_HEADS * 2, N_KEYS, tile), F32)],
        compiler_params=_params("parallel"),
        name="peer_route",
    )(x2, g, wq_t, keys)


def _replicated_row(ref, lead, row):
    return ref[(*lead, pl.ds(row, SUBLANES, stride=0), slice(None))]


def _peer_dense_kernel(final_norm, n_chunks, xn_ref, x_ref, u_ref, vt_ref, r1_ref, c_ref, n2_ref, w_ref,
                       fg_ref, o_ref, acc_ref, h0_ref, h1_ref, a0_ref, a1_ref):
    s = pl.program_id(0)
    tile = xn_ref.shape[1]
    n_pieces = DENSE_PIECES
    up_rows = u_ref.shape[0] // n_pieces
    down_rows = vt_ref.shape[0] // n_pieces
    key_rows = N_KEYS // n_pieces
    rows_per_chunk = u_ref.shape[0] // N_KEYS
    slot = s % 2
    down_chunk = (s - 2) % n_chunks

    @pl.when(s == 0)
    def _():
        h1_ref[...] = jnp.zeros_like(h1_ref)
        a0_ref[...] = jnp.zeros_like(a0_ref)

    @pl.when((s < 2) | (down_chunk == 0))
    def _():
        acc_ref[...] = jnp.zeros_like(acc_ref)

    def run(h_up, h_gate, a_gate, a_down):
        def piece_body(k, carry):
            up = pl.ds(pl.multiple_of(k * up_rows, up_rows), up_rows)
            h_up[up, :] = jnp.dot(u_ref[up, :], xn_ref[...], preferred_element_type=F32)

            keys = pl.ds(pl.multiple_of(k * key_rows, key_rows), key_rows)
            def scalar_rows(ref, hd, j):
                return jnp.concatenate(
                    [pltpu.bitcast(_replicated_row(ref, (hd, lc), j), GATE_DTYPE)
                     for lc in range(tile // LANES)], axis=1)[None]

            def gate_rows(j):
                rows = pl.ds(pl.multiple_of(j * N_KEYS + k * key_rows, key_rows), key_rows)
                h = h_gate[rows, :]
                act = (0.5 * h * (1.0 + lax.erf(h * (1.0 / math.sqrt(2.0))))).astype(GATE_DTYPE)
                slab = (key_rows // BF16_ROWS, BF16_ROWS, tile)
                gate = jnp.zeros(slab, GATE_DTYPE)
                for hd in range(PEER_HEADS):
                    r1 = scalar_rows(r1_ref, hd, j)
                    c = scalar_rows(c_ref, hd, j)
                    n2 = n2_ref[hd, keys, :].reshape(slab)
                    w = w_ref[hd, keys, :].reshape(slab)
                    gate = gate + jnp.where(r1 < n2, w, jnp.zeros_like(w)) * c
                a_gate[rows, :] = (gate.reshape(key_rows, tile) * act).astype(a_gate.dtype)

            for j in range(rows_per_chunk // 2):
                gate_rows(j)
            down = pl.ds(pl.multiple_of(k * down_rows, down_rows), down_rows)
            acc_ref[down, :] += jnp.dot(vt_ref[down, :], a_down[...], preferred_element_type=F32)
            for j in range(rows_per_chunk // 2, rows_per_chunk):
                gate_rows(j)
            return carry

        lax.fori_loop(0, n_pieces, piece_body, 0)

    @pl.when(slot == 0)
    def _():
        run(h0_ref, h1_ref, a1_ref, a0_ref)

    @pl.when(slot == 1)
    def _():
        run(h1_ref, h0_ref, a0_ref, a1_ref)

    @pl.when((s >= 2) & (down_chunk == n_chunks - 1))
    def _():
        y = x_ref[...] + acc_ref[...].T
        if final_norm:
            y = _rmsnorm(y, fg_ref[...])
        o_ref[...] = y


def _peer_dense(xn, x2, u, vt, r1, c, n2, w, final_g, final_norm):
    t, d = x2.shape
    n_experts = u.shape[0]
    tile = min(DENSE_TILE, t)
    ec = EXPERT_CHUNK
    rows = ec // N_KEYS
    n_chunks = n_experts // ec
    n_pos = (t // tile) * n_chunks

    def pos(s, lag):
        p = jnp.clip(s - lag, 0, n_pos - 1)
        return p // n_chunks, p % n_chunks

    up_tile = lambda s: pos(s, 0)[0]
    up_chunk = lambda s: pos(s, 0)[1]
    gate_tile = lambda s: pos(s, 1)[0]
    gate_chunk = lambda s: pos(s, 1)[1]
    down_tile = lambda s: pos(s, 2)[0]
    down_chunk = lambda s: pos(s, 2)[1]
    return pl.pallas_call(
        functools.partial(_peer_dense_kernel, final_norm, n_chunks),
        grid=(n_pos + 2,),
        in_specs=[
            pl.BlockSpec((d, tile), lambda s: (0, up_tile(s))),
            pl.BlockSpec((tile, d), lambda s: (down_tile(s), 0)),
            pl.BlockSpec((ec, d), lambda s: (up_chunk(s), 0)),
            pl.BlockSpec((d, ec), lambda s: (0, down_chunk(s))),
            pl.BlockSpec((PEER_HEADS, tile // LANES, rows, LANES), lambda s: (0, gate_tile(s), gate_chunk(s), 0)),
            pl.BlockSpec((PEER_HEADS, tile // LANES, rows, LANES), lambda s: (0, gate_tile(s), gate_chunk(s), 0)),
            pl.BlockSpec((PEER_HEADS, N_KEYS, tile), lambda s: (0, 0, gate_tile(s))),
            pl.BlockSpec((PEER_HEADS, N_KEYS, tile), lambda s: (0, 0, gate_tile(s))),
            pl.BlockSpec((1, d), lambda s: (0, 0)),
        ],
        out_specs=pl.BlockSpec((tile, d), lambda s: (down_tile(s), 0)),
        out_shape=jax.ShapeDtypeStruct((t, d), F32),
        scratch_shapes=[
            pltpu.VMEM((d, tile), F32),
            pltpu.VMEM((ec, tile), F32),
            pltpu.VMEM((ec, tile), F32),
            pltpu.VMEM((ec, tile), BF16),
            pltpu.VMEM((ec, tile), BF16),
        ],
        compiler_params=_params("arbitrary"),
        name="peer_dense",
    )(xn, x2, u, vt, r1, c, n2, w, final_g)


def _peer(x2, g, w_q, subkeys, u_tab, v_tab, final_g, final_norm):
    d = x2.shape[1]
    wq_t = w_q.T.astype(BF16)
    keys = subkeys.reshape(PEER_HEADS * 2, N_KEYS, SUB_DIM).astype(BF16)
    xn, r1, c, n2, w = _peer_route(x2, g.reshape(1, d), wq_t, keys)
    return _peer_dense(xn, x2, u_tab.astype(BF16), v_tab.T.astype(BF16), r1, c, n2, w,
                       final_g.reshape(1, d), final_norm)


def kernel(x, conv_norm_g, conv_w_in, conv_w, conv_w_out, attn_norm_g, attn_w_qkv, attn_sink, attn_w_o,
           rel_bias, ffn_norm_g, peer_w_q, peer_subkeys, peer_u, peer_v, final_norm_g):
    bsz, seq, d = x.shape
    depth = ffn_norm_g.shape[0]
    x2 = x.reshape(bsz * seq, d)
    bucket = _t5_bucket_table()
    for i in range(depth):
        j = i // 2
        if i % 2 == 0:
            u, gb = _conv_in(x2, conv_norm_g[j].reshape(1, d), conv_w_in[j].astype(BF16))
            x2 = _conv_out(u, gb, x2, conv_w[j], conv_w_out[j].astype(BF16), seq)
        else:
            q, k, v = _attn_qkv(x2, attn_norm_g[j].reshape(1, d), attn_w_qkv[j].astype(BF16))
            x2 = _attn(q, k, v, bucket, rel_bias, attn_sink[j], x2, attn_w_o[j].astype(BF16), seq)
        x2 = _peer(x2, ffn_norm_g[i], peer_w_q[i], peer_subkeys[i], peer_u[i], peer_v[i],
                   final_norm_g, i == depth - 1)
    return x2.reshape(bsz, seq, d)
```

```python
import functools
import math

import jax
import jax.numpy as jnp
import numpy as np
from jax import lax
from jax.experimental import pallas as pl
from jax.experimental.pallas import tpu as pltpu

F32 = jnp.float32
BF16 = jnp.bfloat16
GATE_DTYPE = jnp.bfloat16

RMS_EPS = 1e-6
CONV_WIDTH = 3
N_HEADS = 16
N_KV_HEADS = 4
HEAD_DIM = 64
GROUP = N_HEADS // N_KV_HEADS
WINDOW = 128
BLOCK = 128
NEG_INF = -1e30
N_BUCKETS = 32
MAX_DISTANCE = 128
PEER_HEADS = 8
N_KEYS = 128
PEER_TOPK = 16
SUB_DIM = 128

SUBLANES = 8
LANES = 128
BF16_ROWS = 2 * SUBLANES
MXU_DEPTH = 256
VMEM_LIMIT_BYTES = 52 * 1024 * 1024

CONV_TILE = 512
QKV_TILE = 512
ROUTE_TILE = 256
ROUTE_HEADS_PER_ITER = 4
DENSE_TILE = 512
EXPERT_CHUNK = 1024
DENSE_PIECES = 4
DOT_LAG = 1
SLAB_ROWS = 32


def _params(*semantics, flags=None):
    return pltpu.CompilerParams(dimension_semantics=semantics, vmem_limit_bytes=VMEM_LIMIT_BYTES,
                                flags=flags)


def _rmsnorm(x, g):
    ms = jnp.mean(x * x, axis=-1, keepdims=True)
    return x * lax.rsqrt(ms + RMS_EPS) * g


def _conv_in_kernel(x_ref, g_ref, w_ref, u_ref, gb_ref):
    d = x_ref.shape[1]
    xn = _rmsnorm(x_ref[...], g_ref[...]).astype(BF16)
    gb_ref[...] = jnp.dot(xn, w_ref[:, :d], preferred_element_type=F32)
    gate_c = jnp.dot(xn, w_ref[:, d:2 * d], preferred_element_type=F32)
    h = jnp.dot(xn, w_ref[:, 2 * d:], preferred_element_type=F32)
    u_ref[...] = gate_c * h


def _conv_in(x2, g, w_in):
    t, d = x2.shape
    tile = min(CONV_TILE, t)
    return pl.pallas_call(
        _conv_in_kernel,
        grid=(t // tile,),
        in_specs=[
            pl.BlockSpec((tile, d), lambda i: (i, 0)),
            pl.BlockSpec((1, d), lambda i: (0, 0)),
            pl.BlockSpec((d, 3 * d), lambda i: (0, 0)),
        ],
        out_specs=[
            pl.BlockSpec((tile, d), lambda i: (i, 0)),
            pl.BlockSpec((tile, d), lambda i: (i, 0)),
        ],
        out_shape=[jax.ShapeDtypeStruct((t, d), F32), jax.ShapeDtypeStruct((t, d), F32)],
        compiler_params=_params("parallel"),
        name="conv_in",
    )(x2, g, w_in)


def _conv_out_kernel(tiles_per_seq, u_ref, uprev_ref, unext_ref, gb_ref, x_ref, cw_ref, w_ref, o_ref):
    tile = u_ref.shape[0]
    pos = pl.program_id(0) % tiles_per_seq
    u = u_ref[...]
    prev_row = jnp.where(pos == 0, 0.0, uprev_ref[SUBLANES - 1:SUBLANES, :])
    next_row = jnp.where(pos == tiles_per_seq - 1, 0.0, unext_ref[0:1, :])
    row = lax.broadcasted_iota(jnp.int32, u.shape, 0)
    u_before = jnp.where(row == 0, prev_row, pltpu.roll(u, 1, axis=0))
    u_after = jnp.where(row == tile - 1, next_row, pltpu.roll(u, tile - 1, axis=0))
    y = cw_ref[0:1, :] * u_before + cw_ref[1:2, :] * u + cw_ref[2:3, :] * u_after
    z = (gb_ref[...] * y).astype(BF16)
    o_ref[...] = x_ref[...] + jnp.dot(z, w_ref[...], preferred_element_type=F32)


def _conv_out(u, gb, x2, conv_w, w_out, seq):
    t, d = x2.shape
    tile = min(CONV_TILE, seq)
    halo_per_tile = tile // SUBLANES
    n_halo = t // SUBLANES
    return pl.pallas_call(
        functools.partial(_conv_out_kernel, seq // tile),
        grid=(t // tile,),
        in_specs=[
            pl.BlockSpec((tile, d), lambda i: (i, 0)),
            pl.BlockSpec((SUBLANES, d), lambda i: (jnp.maximum(i * halo_per_tile - 1, 0), 0)),
            pl.BlockSpec((SUBLANES, d), lambda i: (jnp.minimum((i + 1) * halo_per_tile, n_halo - 1), 0)),
            pl.BlockSpec((tile, d), lambda i: (i, 0)),
            pl.BlockSpec((tile, d), lambda i: (i, 0)),
            pl.BlockSpec((CONV_WIDTH, d), lambda i: (0, 0)),
            pl.BlockSpec((d, d), lambda i: (0, 0)),
        ],
        out_specs=pl.BlockSpec((tile, d), lambda i: (i, 0)),
        out_shape=jax.ShapeDtypeStruct((t, d), F32),
        compiler_params=_params("parallel"),
        name="conv_out",
    )(u, u, u, gb, x2, conv_w, w_out)


def _attn_qkv_kernel(x_ref, g_ref, w_ref, q_ref, k_ref, v_ref):
    nq = q_ref.shape[1]
    nk = k_ref.shape[1]
    xn = _rmsnorm(x_ref[...], g_ref[...]).astype(BF16)
    q = jnp.dot(xn, w_ref[:, :nq], preferred_element_type=F32) * (1.0 / math.sqrt(HEAD_DIM))
    q_ref[...] = q.astype(BF16)
    k_ref[...] = jnp.dot(xn, w_ref[:, nq:nq + nk], preferred_element_type=F32).astype(BF16)
    v_ref[...] = jnp.dot(xn, w_ref[:, nq + nk:], preferred_element_type=F32).astype(BF16)


def _attn_qkv(x2, g, w_qkv):
    t, d = x2.shape
    tile = min(QKV_TILE, t)
    nq = N_HEADS * HEAD_DIM
    nk = N_KV_HEADS * HEAD_DIM
    return pl.pallas_call(
        _attn_qkv_kernel,
        grid=(t // tile,),
        in_specs=[
            pl.BlockSpec((tile, d), lambda i: (i, 0)),
            pl.BlockSpec((1, d), lambda i: (0, 0)),
            pl.BlockSpec((d, nq + 2 * nk), lambda i: (0, 0)),
        ],
        out_specs=[
            pl.BlockSpec((tile, nq), lambda i: (i, 0)),
            pl.BlockSpec((tile, nk), lambda i: (i, 0)),
            pl.BlockSpec((tile, nk), lambda i: (i, 0)),
        ],
        out_shape=[
            jax.ShapeDtypeStruct((t, nq), BF16),
            jax.ShapeDtypeStruct((t, nk), BF16),
            jax.ShapeDtypeStruct((t, nk), BF16),
        ],
        compiler_params=_params("parallel"),
        name="attn_qkv",
    )(x2, g, w_qkv)


def _attn_kernel(blocks_per_seq, q_ref, kp_ref, kc_ref, kn_ref, vp_ref, vc_ref, vn_ref, bucket_ref,
                 relb_ref, sink_ref, x_ref, wo_ref, o_ref, bias_ref, att_ref):
    step = pl.program_id(0)

    @pl.when(step == 0)
    def _():
        bucket = bucket_ref[...]
        rel = lax.broadcasted_iota(jnp.int32, bucket.shape, 1) - BLOCK - lax.broadcasted_iota(
            jnp.int32, bucket.shape, 0)
        in_window = jnp.abs(rel) <= WINDOW

        def head_body(hq, carry):
            def bucket_body(b, acc):
                return jnp.where(bucket == b, relb_ref[b, hq], acc)

            table = lax.fori_loop(0, N_BUCKETS, bucket_body, jnp.zeros(bucket.shape, F32))
            bias_ref[hq] = jnp.where(in_window, table, NEG_INF)
            return carry

        lax.fori_loop(0, N_HEADS, head_body, 0)

    pos = step % blocks_per_seq
    has_prev = pos > 0
    has_next = pos < blocks_per_seq - 1
    col = lax.broadcasted_iota(jnp.int32, (BLOCK, 3 * BLOCK), 1)
    key_ok = ((col >= BLOCK) | has_prev) & ((col < 2 * BLOCK) | has_next)

    for kvh in range(N_KV_HEADS):
        lo = kvh * HEAD_DIM
        k = jnp.concatenate([kp_ref[:, lo:lo + HEAD_DIM], kc_ref[:, lo:lo + HEAD_DIM],
                             kn_ref[:, lo:lo + HEAD_DIM]], axis=0)
        v = jnp.concatenate([vp_ref[:, lo:lo + HEAD_DIM], vc_ref[:, lo:lo + HEAD_DIM],
                             vn_ref[:, lo:lo + HEAD_DIM]], axis=0)
        for g in range(GROUP):
            hq = kvh * GROUP + g
            q = q_ref[:, hq * HEAD_DIM:(hq + 1) * HEAD_DIM]
            s = lax.dot_general(q, k, (((1,), (1,)), ((), ())), preferred_element_type=F32)
            logits = jnp.where(key_ok, s + bias_ref[hq], NEG_INF)
            sink = sink_ref[hq]
            m = jnp.maximum(jnp.max(logits, axis=-1, keepdims=True), sink)
            p = jnp.exp(logits - m)
            den = jnp.sum(p, axis=-1, keepdims=True) + jnp.exp(sink - m)
            o = jnp.dot(p.astype(BF16), v, preferred_element_type=F32) / den
            att_ref[:, hq * HEAD_DIM:(hq + 1) * HEAD_DIM] = o.astype(BF16)

    o_ref[...] = x_ref[...] + jnp.dot(att_ref[...], wo_ref[...], preferred_element_type=F32)


def _attn(q, k, v, bucket, rel_bias, sink, x2, w_o, seq):
    t, d = x2.shape
    nq = N_HEADS * HEAD_DIM
    nk = N_KV_HEADS * HEAD_DIM
    n_blocks = t // BLOCK
    prev_map = lambda i: (jnp.maximum(i - 1, 0), 0)
    cur_map = lambda i: (i, 0)
    next_map = lambda i: (jnp.minimum(i + 1, n_blocks - 1), 0)
    smem = pl.BlockSpec(memory_space=pltpu.SMEM)
    return pl.pallas_call(
        functools.partial(_attn_kernel, seq // BLOCK),
        grid=(n_blocks,),
        in_specs=[
            pl.BlockSpec((BLOCK, nq), cur_map),
            pl.BlockSpec((BLOCK, nk), prev_map),
            pl.BlockSpec((BLOCK, nk), cur_map),
            pl.BlockSpec((BLOCK, nk), next_map),
            pl.BlockSpec((BLOCK, nk), prev_map),
            pl.BlockSpec((BLOCK, nk), cur_map),
            pl.BlockSpec((BLOCK, nk), next_map),
            pl.BlockSpec((BLOCK, 3 * BLOCK), lambda i: (0, 0)),
            smem,
            smem,
            pl.BlockSpec((BLOCK, d), cur_map),
            pl.BlockSpec((nq, d), lambda i: (0, 0)),
        ],
        out_specs=pl.BlockSpec((BLOCK, d), cur_map),
        out_shape=jax.ShapeDtypeStruct((t, d), F32),
        scratch_shapes=[
            pltpu.VMEM((N_HEADS, BLOCK, 3 * BLOCK), F32),
            pltpu.VMEM((BLOCK, nq), BF16),
        ],
        compiler_params=_params("arbitrary"),
        name="attn",
    )(q, k, k, k, v, v, v, bucket, rel_bias, sink, x2, w_o)


def _t5_bucket_table():
    half = N_BUCKETS // 2
    max_exact = half // 2
    qi = jnp.arange(BLOCK)[:, None]
    kj = jnp.arange(3 * BLOCK)[None, :]
    rel = kj - BLOCK - qi
    ret = jnp.where(rel > 0, half, 0)
    n = jnp.abs(rel)
    nf = jnp.maximum(n, 1).astype(jnp.float32)
    large = max_exact + (jnp.log(nf / max_exact) / math.log(MAX_DISTANCE / max_exact)
                         * (half - max_exact)).astype(jnp.int32)
    large = jnp.minimum(large, half - 1)
    return (ret + jnp.where(n < max_exact, n, large)).astype(jnp.int32)


INT32_MIN = -(2 ** 31)


def _ordered_int(bits):
    return bits ^ ((bits >> 31) & 0x7FFFFFFF)


def _top16_ranks(s, key_iota):
    bits = lax.bitcast_convert_type(s, jnp.int32)
    keys = _ordered_int(jnp.where(bits == INT32_MIN, 0, bits))
    slot = lax.broadcasted_iota(jnp.int32, (PEER_TOPK, s.shape[1]), 0)
    top = jnp.zeros((PEER_TOPK, s.shape[1]), jnp.int32)
    for r in range(PEER_TOPK):
        m = jnp.max(keys, axis=0, keepdims=True)
        first = jnp.min(jnp.where(keys == m, key_iota, N_KEYS), axis=0, keepdims=True)
        keys = jnp.where(key_iota == first, INT32_MIN + r, keys)
        top = jnp.where(slot == r, m, top)
    rank = jnp.where(keys < INT32_MIN + PEER_TOPK, keys ^ INT32_MIN, PEER_TOPK).astype(F32)
    return rank, lax.bitcast_convert_type(_ordered_int(top), F32)


def _staircase(sv1, sv2):
    lanes = sv1.shape[1]
    slot = lax.broadcasted_iota(jnp.int32, (PEER_TOPK, lanes), 0)
    n = jnp.zeros((PEER_TOPK, lanes), jnp.int32)
    front = sv1[0:1, :] + sv2
    big = PEER_TOPK * PEER_TOPK + PEER_TOPK
    for r in range(PEER_TOPK):
        m = jnp.max(front, axis=0, keepdims=True)
        flat = n * PEER_TOPK + slot
        first = jnp.min(jnp.where(front == m, flat, big), axis=0, keepdims=True)
        chosen = flat == first
        n = n + chosen.astype(jnp.int32)
        if r + 1 < PEER_TOPK:
            depth = jnp.sum(jnp.where(chosen, n, 0), axis=0, keepdims=True)
            nxt = jnp.sum(jnp.where(slot == depth, sv1, 0.0), axis=0, keepdims=True)
            front = jnp.where(chosen, jnp.where(depth < PEER_TOPK, nxt + sv2, -jnp.inf), front)
    return n.astype(F32)


def _twice_bf16(v):
    bits = lax.bitcast_convert_type(v.astype(GATE_DTYPE).astype(F32), jnp.uint32)
    return lax.bitcast_convert_type(bits | (bits >> 16), jnp.int32)


def _peer_route_kernel(x_ref, g_ref, wq_ref, keys_ref, xn_ref, r1_ref, c_ref, n2_ref, w_ref, s_ref):
    tile = x_ref.shape[0]
    xn_t = _rmsnorm(x_ref[...], g_ref[...]).T.astype(BF16)
    xn_ref[...] = xn_t
    q_t = jnp.dot(wq_ref[...], xn_t, preferred_element_type=F32)
    for hc in range(PEER_HEADS * 2):
        q_hc = q_t[hc * SUB_DIM:(hc + 1) * SUB_DIM, :].astype(BF16)
        s_ref[hc] = jnp.dot(keys_ref[hc], q_hc, preferred_element_type=F32)

    key_iota = lax.broadcasted_iota(jnp.int32, (N_KEYS, LANES), 0)
    slot = lax.broadcasted_iota(jnp.int32, (PEER_TOPK, LANES), 0).astype(F32)

    def route_head(hd, lane_chunk):
        lane0 = pl.multiple_of(lane_chunk * LANES, LANES)
        lanes = pl.ds(lane0, LANES)
        s1 = s_ref[2 * hd, :, lanes]
        s2 = s_ref[2 * hd + 1, :, lanes]
        rank1, sv1 = _top16_ranks(s1, key_iota)
        rank2, sv2 = _top16_ranks(s2, key_iota)
        n = _staircase(sv1, sv2)
        e1 = jnp.exp(s1 - sv1[0:1, :])
        e2 = jnp.exp(s2 - sv2[0:1, :])
        e1s = jnp.exp(sv1 - sv1[0:1, :])
        e2s = jnp.exp(sv2 - sv2[0:1, :])
        inner = jnp.zeros_like(e2s)
        for p in range(PEER_TOPK):
            inner = inner + jnp.where(n > float(p), e1s[p:p + 1, :], 0.0)
        z = jnp.sum(inner * e2s, axis=0, keepdims=True)
        n2 = jnp.zeros_like(rank2)
        for q in range(PEER_TOPK):
            n2 = jnp.where(rank2 == float(q), n[q:q + 1, :], n2)
        r1_ref[hd, lane_chunk] = _twice_bf16(rank1)
        c_ref[hd, lane_chunk] = _twice_bf16(e1 / z)
        n2_ref[hd, :, lanes] = n2.astype(GATE_DTYPE)
        w_ref[hd, :, lanes] = e2.astype(GATE_DTYPE)

    def body(it, carry):
        lane_chunk = it % (tile // LANES)
        for i in range(ROUTE_HEADS_PER_ITER):
            route_head((it // (tile // LANES)) * ROUTE_HEADS_PER_ITER + i, lane_chunk)
        return carry

    lax.fori_loop(0, (PEER_HEADS // ROUTE_HEADS_PER_ITER) * (tile // LANES), body, 0)


def _peer_route(x2, g, wq_t, keys):
    t, d = x2.shape
    tile = min(ROUTE_TILE, t)
    nqd = PEER_HEADS * 2 * SUB_DIM
    scalar_shape = jax.ShapeDtypeStruct((PEER_HEADS, t // LANES, N_KEYS, LANES), jnp.int32)
    scalar_spec = pl.BlockSpec((PEER_HEADS, tile // LANES, N_KEYS, LANES), lambda i: (0, i, 0, 0))
    table_shape = jax.ShapeDtypeStruct((PEER_HEADS, N_KEYS, t), GATE_DTYPE)
    table_spec = pl.BlockSpec((PEER_HEADS, N_KEYS, tile), lambda i: (0, 0, i))
    return pl.pallas_call(
        _peer_route_kernel,
        grid=(t // tile,),
        in_specs=[
            pl.BlockSpec((tile, d), lambda i: (i, 0)),
            pl.BlockSpec((1, d), lambda i: (0, 0)),
            pl.BlockSpec((nqd, d), lambda i: (0, 0)),
            pl.BlockSpec((PEER_HEADS * 2, N_KEYS, SUB_DIM), lambda i: (0, 0, 0)),
        ],
        out_specs=[pl.BlockSpec((d, tile), lambda i: (0, i)), scalar_spec, scalar_spec, table_spec, table_spec],
        out_shape=[jax.ShapeDtypeStruct((d, t), BF16), scalar_shape, scalar_shape, table_shape, table_shape],
        scratch_shapes=[pltpu.VMEM((PEER_HEADS * 2, N_KEYS, tile), F32)],
        compiler_params=_params("parallel"),
        name="peer_route",
    )(x2, g, wq_t, keys)


def _plane_dot(lhs_ref, rows, rhs_ref, p, after=None):
    lhs = lhs_ref[p, rows, :]
    if after is not None:
        words = pltpu.bitcast(lhs, jnp.int32)
        reps = (words.shape[0] // after.shape[0], words.shape[1] // after.shape[1])
        lhs = pltpu.bitcast(words | jnp.tile(after, reps), lhs.dtype)
    return jnp.dot(lhs, rhs_ref[p * MXU_DEPTH:(p + 1) * MXU_DEPTH, :], preferred_element_type=F32)


def _k_planes(table):
    m, k = table.shape
    return table.reshape(m, k // MXU_DEPTH, MXU_DEPTH).transpose(1, 0, 2).astype(BF16)


def _zero_after(v):
    bits = lax.bitcast_convert_type(v, jnp.uint32)
    return lax.bitcast_convert_type((bits >> 16) >> 16, jnp.int32)


def _replicated_row(ref, lead, row):
    return ref[(*lead, pl.ds(row, SUBLANES, stride=0), slice(None))]


def _peer_dense_kernel(final_norm, n_chunks, xn_ref, x_ref, u_ref, vt_ref, r1_ref, c_ref, n2_ref, w_ref,
                       fg_ref, o_ref, acc_ref, h0_ref, h1_ref, a0_ref, a1_ref):
    s = pl.program_id(0)
    tile = xn_ref.shape[1]
    n_pieces = DENSE_PIECES
    up_rows = u_ref.shape[1] // n_pieces
    down_rows = vt_ref.shape[1] // n_pieces
    key_rows = N_KEYS // n_pieces
    rows_per_chunk = u_ref.shape[1] // N_KEYS
    slot = s % 2
    down_chunk = (s - 2) % n_chunks

    @pl.when(s == 0)
    def _():
        h1_ref[...] = jnp.zeros_like(h1_ref)
        a0_ref[...] = jnp.zeros_like(a0_ref)

    @pl.when((s < 2) | (down_chunk == 0))
    def _():
        acc_ref[...] = jnp.zeros_like(acc_ref)

    def run(h_up, h_gate, a_gate, a_down):
        def piece_body(k, carry):
            up = pl.ds(pl.multiple_of(k * up_rows, up_rows), up_rows)
            down = pl.ds(pl.multiple_of(k * down_rows, down_rows), down_rows)
            n_planes = u_ref.shape[0]
            slabs = [(j, sub) for j in range(rows_per_chunk) for sub in range(key_rows // SLAB_ROWS)]
            slabs_per_dot = len(slabs) // (2 * n_planes)
            slab = (SLAB_ROWS // BF16_ROWS, BF16_ROWS, tile)

            def scalar_rows(ref, hd, j, anchor):
                words = [_replicated_row(ref, (hd, lc), j) for lc in range(tile // LANES)]
                if anchor is not None:
                    words[-1] = words[-1] | anchor
                return jnp.concatenate([pltpu.bitcast(wd, GATE_DTYPE) for wd in words], axis=1)[None]

            def gate_slab(j, sub, anchor):
                key0 = pl.multiple_of(k * key_rows + sub * SLAB_ROWS, SLAB_ROWS)
                keys = pl.ds(key0, SLAB_ROWS)
                rows = pl.ds(pl.multiple_of(j * N_KEYS + key0, SLAB_ROWS), SLAB_ROWS)
                h = h_gate[rows, :]
                act = (0.5 * h * (1.0 + lax.erf(h * (1.0 / math.sqrt(2.0))))).astype(GATE_DTYPE)
                gate = jnp.zeros(slab, GATE_DTYPE)
                for hd in range(PEER_HEADS):
                    r1 = scalar_rows(r1_ref, hd, j, anchor if hd == PEER_HEADS - 1 else None)
                    c = scalar_rows(c_ref, hd, j, None)
                    n2 = n2_ref[hd, keys, :].reshape(slab)
                    w = w_ref[hd, keys, :].reshape(slab)
                    gate = gate + jnp.where(r1 < n2, w, jnp.zeros_like(w)) * c
                out = (gate.reshape(SLAB_ROWS, tile) * act).astype(a_gate.dtype)
                a_gate[rows, :] = out
                return _zero_after(pltpu.bitcast(out[:, :LANES], jnp.int32))

            dots = ([("down", vt_ref, down, a_down, p) for p in range(n_planes)]
                    + [("up", u_ref, up, xn_ref, p) for p in range(n_planes)])
            sums = {}
            tokens = []
            for i, (name, lhs_ref, rows, rhs_ref, p) in enumerate(dots):
                after = tokens[i - DOT_LAG] if i >= DOT_LAG else None
                part = _plane_dot(lhs_ref, rows, rhs_ref, p, after)
                sums[name] = part if p == 0 else sums[name] + part
                if (name, p) == ("up", n_planes - 1):
                    h_up[up, :] = sums["up"]
                if (name, p) == ("down", n_planes - 1):
                    acc_ref[down, :] += sums["down"]
                for j, sub in slabs[i * slabs_per_dot:(i + 1) * slabs_per_dot]:
                    anchor = None
                    if (j, sub) == slabs[-1]:
                        anchor = (_zero_after(sums["up"][-SUBLANES:, :LANES])
                                  | _zero_after(sums["up"][-SUBLANES:, -LANES:])
                                  | _zero_after(sums["down"][-SUBLANES:, :LANES])
                                  | _zero_after(sums["down"][-SUBLANES:, -LANES:]))
                    token = gate_slab(j, sub, anchor)
                tokens.append(token)
            return carry

        lax.fori_loop(0, n_pieces, piece_body, 0)

    @pl.when(slot == 0)
    def _():
        run(h0_ref, h1_ref, a1_ref, a0_ref)

    @pl.when(slot == 1)
    def _():
        run(h1_ref, h0_ref, a0_ref, a1_ref)

    @pl.when((s >= 2) & (down_chunk == n_chunks - 1))
    def _():
        y = x_ref[...] + acc_ref[...].T
        if final_norm:
            y = _rmsnorm(y, fg_ref[...])
        o_ref[...] = y


def _peer_dense(xn, x2, u, vt, r1, c, n2, w, final_g, final_norm):
    t, d = x2.shape
    n_experts = u.shape[1]
    tile = min(DENSE_TILE, t)
    ec = EXPERT_CHUNK
    rows = ec // N_KEYS
    n_chunks = n_experts // ec
    n_pos = (t // tile) * n_chunks

    def pos(s, lag):
        p = jnp.clip(s - lag, 0, n_pos - 1)
        return p // n_chunks, p % n_chunks

    up_tile = lambda s: pos(s, 0)[0]
    up_chunk = lambda s: pos(s, 0)[1]
    gate_tile = lambda s: pos(s, 1)[0]
    gate_chunk = lambda s: pos(s, 1)[1]
    down_tile = lambda s: pos(s, 2)[0]
    down_chunk = lambda s: pos(s, 2)[1]
    return pl.pallas_call(
        functools.partial(_peer_dense_kernel, final_norm, n_chunks),
        grid=(n_pos + 2,),
        in_specs=[
            pl.BlockSpec((d, tile), lambda s: (0, up_tile(s))),
            pl.BlockSpec((tile, d), lambda s: (down_tile(s), 0)),
            pl.BlockSpec((d // MXU_DEPTH, ec, MXU_DEPTH), lambda s: (0, up_chunk(s), 0)),
            pl.BlockSpec((ec // MXU_DEPTH, d, MXU_DEPTH), lambda s: (down_chunk(s), 0, 0)),
            pl.BlockSpec((PEER_HEADS, tile // LANES, rows, LANES), lambda s: (0, gate_tile(s), gate_chunk(s), 0)),
            pl.BlockSpec((PEER_HEADS, tile // LANES, rows, LANES), lambda s: (0, gate_tile(s), gate_chunk(s), 0)),
            pl.BlockSpec((PEER_HEADS, N_KEYS, tile), lambda s: (0, 0, gate_tile(s))),
            pl.BlockSpec((PEER_HEADS, N_KEYS, tile), lambda s: (0, 0, gate_tile(s))),
            pl.BlockSpec((1, d), lambda s: (0, 0)),
        ],
        out_specs=pl.BlockSpec((tile, d), lambda s: (down_tile(s), 0)),
        out_shape=jax.ShapeDtypeStruct((t, d), F32),
        scratch_shapes=[
            pltpu.VMEM((d, tile), F32),
            pltpu.VMEM((ec, tile), F32),
            pltpu.VMEM((ec, tile), F32),
            pltpu.VMEM((ec, tile), BF16),
            pltpu.VMEM((ec, tile), BF16),
        ],
        compiler_params=_params("arbitrary"),
        name="peer_dense",
    )(xn, x2, u, vt, r1, c, n2, w, final_g)


def _peer(x2, g, w_q, subkeys, u_tab, v_tab, final_g, final_norm):
    d = x2.shape[1]
    wq_t = w_q.T.astype(BF16)
    keys = subkeys.reshape(PEER_HEADS * 2, N_KEYS, SUB_DIM).astype(BF16)
    xn, r1, c, n2, w = _peer_route(x2, g.reshape(1, d), wq_t, keys)
    return _peer_dense(xn, x2, _k_planes(u_tab), _k_planes(v_tab.T), r1, c, n2, w,
                       final_g.reshape(1, d), final_norm)


def kernel(x, conv_norm_g, conv_w_in, conv_w, conv_w_out, attn_norm_g, attn_w_qkv, attn_sink, attn_w_o,
           rel_bias, ffn_norm_g, peer_w_q, peer_subkeys, peer_u, peer_v, final_norm_g):
    bsz, seq, d = x.shape
    depth = ffn_norm_g.shape[0]
    x2 = x.reshape(bsz * seq, d)
    bucket = _t5_bucket_table()
    for i in range(depth):
        j = i // 2
        if i % 2 == 0:
            u, gb = _conv_in(x2, conv_norm_g[j].reshape(1, d), conv_w_in[j].astype(BF16))
            x2 = _conv_out(u, gb, x2, conv_w[j], conv_w_out[j].astype(BF16), seq)
        else:
            q, k, v = _attn_qkv(x2, attn_norm_g[j].reshape(1, d), attn_w_qkv[j].astype(BF16))
            x2 = _attn(q, k, v, bucket, rel_bias, attn_sink[j], x2, attn_w_o[j].astype(BF16), seq)
        x2 = _peer(x2, ffn_norm_g[i], peer_w_q[i], peer_subkeys[i], peer_u[i], peer_v[i],
                   final_norm_g, i == depth - 1)
    return x2.reshape(bsz, seq, d)
```

```python
import functools
import math

import jax
import jax.numpy as jnp
import numpy as np
from jax import lax
from jax.experimental import pallas as pl
from jax.experimental.pallas import tpu as pltpu

F32 = jnp.float32
BF16 = jnp.bfloat16
GATE_DTYPE = jnp.bfloat16

RMS_EPS = 1e-6
CONV_WIDTH = 3
N_HEADS = 16
N_KV_HEADS = 4
HEAD_DIM = 64
GROUP = N_HEADS // N_KV_HEADS
WINDOW = 128
BLOCK = 128
NEG_INF = -1e30
N_BUCKETS = 32
MAX_DISTANCE = 128
PEER_HEADS = 8
N_KEYS = 128
PEER_TOPK = 16
SUB_DIM = 128

SUBLANES = 8
LANES = 128
BF16_ROWS = 2 * SUBLANES
MXU_DEPTH = 256
VMEM_LIMIT_BYTES = 52 * 1024 * 1024

CONV_TILE = 512
QKV_TILE = 512
ROUTE_TILE = 256
ROUTE_HEADS_PER_ITER = 4
DENSE_TILE = 512
EXPERT_CHUNK = 1024
DENSE_PIECES = 4
DOT_LAG = 1
SLAB_ROWS = 32


def _params(*semantics, flags=None):
    return pltpu.CompilerParams(dimension_semantics=semantics, vmem_limit_bytes=VMEM_LIMIT_BYTES,
                                flags=flags)


def _rmsnorm(x, g):
    ms = jnp.mean(x * x, axis=-1, keepdims=True)
    return x * lax.rsqrt(ms + RMS_EPS) * g


def _conv_in_kernel(x_ref, g_ref, w_ref, u_ref, gb_ref):
    d = x_ref.shape[1]
    xn = _rmsnorm(x_ref[...], g_ref[...]).astype(BF16)
    gb_ref[...] = jnp.dot(xn, w_ref[:, :d], preferred_element_type=F32)
    gate_c = jnp.dot(xn, w_ref[:, d:2 * d], preferred_element_type=F32)
    h = jnp.dot(xn, w_ref[:, 2 * d:], preferred_element_type=F32)
    u_ref[...] = gate_c * h


def _conv_in(x2, g, w_in):
    t, d = x2.shape
    tile = min(CONV_TILE, t)
    return pl.pallas_call(
        _conv_in_kernel,
        grid=(t // tile,),
        in_specs=[
            pl.BlockSpec((tile, d), lambda i: (i, 0)),
            pl.BlockSpec((1, d), lambda i: (0, 0)),
            pl.BlockSpec((d, 3 * d), lambda i: (0, 0)),
        ],
        out_specs=[
            pl.BlockSpec((tile, d), lambda i: (i, 0)),
            pl.BlockSpec((tile, d), lambda i: (i, 0)),
        ],
        out_shape=[jax.ShapeDtypeStruct((t, d), F32), jax.ShapeDtypeStruct((t, d), F32)],
        compiler_params=_params("parallel"),
        name="conv_in",
    )(x2, g, w_in)


def _conv_out_kernel(tiles_per_seq, u_ref, uprev_ref, unext_ref, gb_ref, x_ref, cw_ref, w_ref, o_ref):
    tile = u_ref.shape[0]
    pos = pl.program_id(0) % tiles_per_seq
    u = u_ref[...]
    prev_row = jnp.where(pos == 0, 0.0, uprev_ref[SUBLANES - 1:SUBLANES, :])
    next_row = jnp.where(pos == tiles_per_seq - 1, 0.0, unext_ref[0:1, :])
    row = lax.broadcasted_iota(jnp.int32, u.shape, 0)
    u_before = jnp.where(row == 0, prev_row, pltpu.roll(u, 1, axis=0))
    u_after = jnp.where(row == tile - 1, next_row, pltpu.roll(u, tile - 1, axis=0))
    y = cw_ref[0:1, :] * u_before + cw_ref[1:2, :] * u + cw_ref[2:3, :] * u_after
    z = (gb_ref[...] * y).astype(BF16)
    o_ref[...] = x_ref[...] + jnp.dot(z, w_ref[...], preferred_element_type=F32)


def _conv_out(u, gb, x2, conv_w, w_out, seq):
    t, d = x2.shape
    tile = min(CONV_TILE, seq)
    halo_per_tile = tile // SUBLANES
    n_halo = t // SUBLANES
    return pl.pallas_call(
        functools.partial(_conv_out_kernel, seq // tile),
        grid=(t // tile,),
        in_specs=[
            pl.BlockSpec((tile, d), lambda i: (i, 0)),
            pl.BlockSpec((SUBLANES, d), lambda i: (jnp.maximum(i * halo_per_tile - 1, 0), 0)),
            pl.BlockSpec((SUBLANES, d), lambda i: (jnp.minimum((i + 1) * halo_per_tile, n_halo - 1), 0)),
            pl.BlockSpec((tile, d), lambda i: (i, 0)),
            pl.BlockSpec((tile, d), lambda i: (i, 0)),
            pl.BlockSpec((CONV_WIDTH, d), lambda i: (0, 0)),
            pl.BlockSpec((d, d), lambda i: (0, 0)),
        ],
        out_specs=pl.BlockSpec((tile, d), lambda i: (i, 0)),
        out_shape=jax.ShapeDtypeStruct((t, d), F32),
        compiler_params=_params("parallel"),
        name="conv_out",
    )(u, u, u, gb, x2, conv_w, w_out)


def _attn_qkv_kernel(x_ref, g_ref, w_ref, q_ref, k_ref, v_ref):
    nq = q_ref.shape[1]
    nk = k_ref.shape[1]
    xn = _rmsnorm(x_ref[...], g_ref[...]).astype(BF16)
    q = jnp.dot(xn, w_ref[:, :nq], preferred_element_type=F32) * (1.0 / math.sqrt(HEAD_DIM))
    q_ref[...] = q.astype(BF16)
    k_ref[...] = jnp.dot(xn, w_ref[:, nq:nq + nk], preferred_element_type=F32).astype(BF16)
    v_ref[...] = jnp.dot(xn, w_ref[:, nq + nk:], preferred_element_type=F32).astype(BF16)


def _attn_qkv(x2, g, w_qkv):
    t, d = x2.shape
    tile = min(QKV_TILE, t)
    nq = N_HEADS * HEAD_DIM
    nk = N_KV_HEADS * HEAD_DIM
    return pl.pallas_call(
        _attn_qkv_kernel,
        grid=(t // tile,),
        in_specs=[
            pl.BlockSpec((tile, d), lambda i: (i, 0)),
            pl.BlockSpec((1, d), lambda i: (0, 0)),
            pl.BlockSpec((d, nq + 2 * nk), lambda i: (0, 0)),
        ],
        out_specs=[
            pl.BlockSpec((tile, nq), lambda i: (i, 0)),
            pl.BlockSpec((tile, nk), lambda i: (i, 0)),
            pl.BlockSpec((tile, nk), lambda i: (i, 0)),
        ],
        out_shape=[
            jax.ShapeDtypeStruct((t, nq), BF16),
            jax.ShapeDtypeStruct((t, nk), BF16),
            jax.ShapeDtypeStruct((t, nk), BF16),
        ],
        compiler_params=_params("parallel"),
        name="attn_qkv",
    )(x2, g, w_qkv)


def _attn_kernel(blocks_per_seq, q_ref, kp_ref, kc_ref, kn_ref, vp_ref, vc_ref, vn_ref, bucket_ref,
                 relb_ref, sink_ref, x_ref, wo_ref, o_ref, bias_ref, att_ref):
    step = pl.program_id(0)

    @pl.when(step == 0)
    def _():
        bucket = bucket_ref[...]
        rel = lax.broadcasted_iota(jnp.int32, bucket.shape, 1) - BLOCK - lax.broadcasted_iota(
            jnp.int32, bucket.shape, 0)
        in_window = jnp.abs(rel) <= WINDOW

        def head_body(hq, carry):
            def bucket_body(b, acc):
                return jnp.where(bucket == b, relb_ref[b, hq], acc)

            table = lax.fori_loop(0, N_BUCKETS, bucket_body, jnp.zeros(bucket.shape, F32))
            bias_ref[hq] = jnp.where(in_window, table, NEG_INF)
            return carry

        lax.fori_loop(0, N_HEADS, head_body, 0)

    pos = step % blocks_per_seq
    has_prev = pos > 0
    has_next = pos < blocks_per_seq - 1
    col = lax.broadcasted_iota(jnp.int32, (BLOCK, 3 * BLOCK), 1)
    key_ok = ((col >= BLOCK) | has_prev) & ((col < 2 * BLOCK) | has_next)

    for kvh in range(N_KV_HEADS):
        lo = kvh * HEAD_DIM
        k = jnp.concatenate([kp_ref[:, lo:lo + HEAD_DIM], kc_ref[:, lo:lo + HEAD_DIM],
                             kn_ref[:, lo:lo + HEAD_DIM]], axis=0)
        v = jnp.concatenate([vp_ref[:, lo:lo + HEAD_DIM], vc_ref[:, lo:lo + HEAD_DIM],
                             vn_ref[:, lo:lo + HEAD_DIM]], axis=0)
        for g in range(GROUP):
            hq = kvh * GROUP + g
            q = q_ref[:, hq * HEAD_DIM:(hq + 1) * HEAD_DIM]
            s = lax.dot_general(q, k, (((1,), (1,)), ((), ())), preferred_element_type=F32)
            logits = jnp.where(key_ok, s + bias_ref[hq], NEG_INF)
            sink = sink_ref[hq]
            m = jnp.maximum(jnp.max(logits, axis=-1, keepdims=True), sink)
            p = jnp.exp(logits - m)
            den = jnp.sum(p, axis=-1, keepdims=True) + jnp.exp(sink - m)
            o = jnp.dot(p.astype(BF16), v, preferred_element_type=F32) / den
            att_ref[:, hq * HEAD_DIM:(hq + 1) * HEAD_DIM] = o.astype(BF16)

    o_ref[...] = x_ref[...] + jnp.dot(att_ref[...], wo_ref[...], preferred_element_type=F32)


def _attn(q, k, v, bucket, rel_bias, sink, x2, w_o, seq):
    t, d = x2.shape
    nq = N_HEADS * HEAD_DIM
    nk = N_KV_HEADS * HEAD_DIM
    n_blocks = t // BLOCK
    prev_map = lambda i: (jnp.maximum(i - 1, 0), 0)
    cur_map = lambda i: (i, 0)
    next_map = lambda i: (jnp.minimum(i + 1, n_blocks - 1), 0)
    smem = pl.BlockSpec(memory_space=pltpu.SMEM)
    return pl.pallas_call(
        functools.partial(_attn_kernel, seq // BLOCK),
        grid=(n_blocks,),
        in_specs=[
            pl.BlockSpec((BLOCK, nq), cur_map),
            pl.BlockSpec((BLOCK, nk), prev_map),
            pl.BlockSpec((BLOCK, nk), cur_map),
            pl.BlockSpec((BLOCK, nk), next_map),
            pl.BlockSpec((BLOCK, nk), prev_map),
            pl.BlockSpec((BLOCK, nk), cur_map),
            pl.BlockSpec((BLOCK, nk), next_map),
            pl.BlockSpec((BLOCK, 3 * BLOCK), lambda i: (0, 0)),
            smem,
            smem,
            pl.BlockSpec((BLOCK, d), cur_map),
            pl.BlockSpec((nq, d), lambda i: (0, 0)),
        ],
        out_specs=pl.BlockSpec((BLOCK, d), cur_map),
        out_shape=jax.ShapeDtypeStruct((t, d), F32),
        scratch_shapes=[
            pltpu.VMEM((N_HEADS, BLOCK, 3 * BLOCK), F32),
            pltpu.VMEM((BLOCK, nq), BF16),
        ],
        compiler_params=_params("arbitrary"),
        name="attn",
    )(q, k, k, k, v, v, v, bucket, rel_bias, sink, x2, w_o)


def _t5_bucket_table():
    half = N_BUCKETS // 2
    max_exact = half // 2
    qi = jnp.arange(BLOCK)[:, None]
    kj = jnp.arange(3 * BLOCK)[None, :]
    rel = kj - BLOCK - qi
    ret = jnp.where(rel > 0, half, 0)
    n = jnp.abs(rel)
    nf = jnp.maximum(n, 1).astype(jnp.float32)
    large = max_exact + (jnp.log(nf / max_exact) / math.log(MAX_DISTANCE / max_exact)
                         * (half - max_exact)).astype(jnp.int32)
    large = jnp.minimum(large, half - 1)
    return (ret + jnp.where(n < max_exact, n, large)).astype(jnp.int32)


INT32_MIN = -(2 ** 31)


def _ordered_int(bits):
    return bits ^ ((bits >> 31) & 0x7FFFFFFF)


def _top16_ranks(s, key_iota):
    bits = lax.bitcast_convert_type(s, jnp.int32)
    keys = _ordered_int(jnp.where(bits == INT32_MIN, 0, bits))
    slot = lax.broadcasted_iota(jnp.int32, (PEER_TOPK, s.shape[1]), 0)
    top = jnp.zeros((PEER_TOPK, s.shape[1]), jnp.int32)
    for r in range(PEER_TOPK):
        m = jnp.max(keys, axis=0, keepdims=True)
        first = jnp.min(jnp.where(keys == m, key_iota, N_KEYS), axis=0, keepdims=True)
        keys = jnp.where(key_iota == first, INT32_MIN + r, keys)
        top = jnp.where(slot == r, m, top)
    rank = jnp.where(keys < INT32_MIN + PEER_TOPK, keys ^ INT32_MIN, PEER_TOPK).astype(F32)
    return rank, lax.bitcast_convert_type(_ordered_int(top), F32)


def _staircase(sv1, sv2):
    lanes = sv1.shape[1]
    slot = lax.broadcasted_iota(jnp.int32, (PEER_TOPK, lanes), 0)
    n = jnp.zeros((PEER_TOPK, lanes), jnp.int32)
    front = sv1[0:1, :] + sv2
    big = PEER_TOPK * PEER_TOPK + PEER_TOPK
    for r in range(PEER_TOPK):
        m = jnp.max(front, axis=0, keepdims=True)
        flat = n * PEER_TOPK + slot
        first = jnp.min(jnp.where(front == m, flat, big), axis=0, keepdims=True)
        chosen = flat == first
        n = n + chosen.astype(jnp.int32)
        if r + 1 < PEER_TOPK:
            depth = jnp.sum(jnp.where(chosen, n, 0), axis=0, keepdims=True)
            nxt = jnp.sum(jnp.where(slot == depth, sv1, 0.0), axis=0, keepdims=True)
            front = jnp.where(chosen, jnp.where(depth < PEER_TOPK, nxt + sv2, -jnp.inf), front)
    return n.astype(F32)


def _twice_bf16(v):
    bits = lax.bitcast_convert_type(v.astype(GATE_DTYPE).astype(F32), jnp.uint32)
    return lax.bitcast_convert_type(bits | (bits >> 16), jnp.int32)


def _peer_route_kernel(x_ref, g_ref, wq_ref, keys_ref, xn_ref, r1_ref, c_ref, n2_ref, w_ref, s_ref):
    tile = x_ref.shape[0]
    xn_t = _rmsnorm(x_ref[...], g_ref[...]).T.astype(BF16)
    xn_ref[...] = xn_t
    q_t = jnp.dot(wq_ref[...], xn_t, preferred_element_type=F32)
    for hc in range(PEER_HEADS * 2):
        q_hc = q_t[hc * SUB_DIM:(hc + 1) * SUB_DIM, :].astype(BF16)
        s_ref[hc] = jnp.dot(keys_ref[hc], q_hc, preferred_element_type=F32)

    key_iota = lax.broadcasted_iota(jnp.int32, (N_KEYS, LANES), 0)
    slot = lax.broadcasted_iota(jnp.int32, (PEER_TOPK, LANES), 0).astype(F32)

    def route_head(hd, lane_chunk):
        lane0 = pl.multiple_of(lane_chunk * LANES, LANES)
        lanes = pl.ds(lane0, LANES)
        s1 = s_ref[2 * hd, :, lanes]
        s2 = s_ref[2 * hd + 1, :, lanes]
        rank1, sv1 = _top16_ranks(s1, key_iota)
        rank2, sv2 = _top16_ranks(s2, key_iota)
        n = _staircase(sv1, sv2)
        e1 = jnp.exp(s1 - sv1[0:1, :])
        e2 = jnp.exp(s2 - sv2[0:1, :])
        e1s = jnp.exp(sv1 - sv1[0:1, :])
        e2s = jnp.exp(sv2 - sv2[0:1, :])
        inner = jnp.zeros_like(e2s)
        for p in range(PEER_TOPK):
            inner = inner + jnp.where(n > float(p), e1s[p:p + 1, :], 0.0)
        z = jnp.sum(inner * e2s, axis=0, keepdims=True)
        n2 = jnp.zeros_like(rank2)
        for q in range(PEER_TOPK):
            n2 = jnp.where(rank2 == float(q), n[q:q + 1, :], n2)
        r1_ref[hd, lane_chunk] = _twice_bf16(rank1)
        c_ref[hd, lane_chunk] = _twice_bf16(e1 / z)
        n2_ref[hd, :, lanes] = n2.astype(GATE_DTYPE)
        w_ref[hd, :, lanes] = e2.astype(GATE_DTYPE)

    def body(it, carry):
        lane_chunk = it % (tile // LANES)
        for i in range(ROUTE_HEADS_PER_ITER):
            route_head((it // (tile // LANES)) * ROUTE_HEADS_PER_ITER + i, lane_chunk)
        return carry

    lax.fori_loop(0, (PEER_HEADS // ROUTE_HEADS_PER_ITER) * (tile // LANES), body, 0)


def _peer_route(x2, g, wq_t, keys):
    t, d = x2.shape
    tile = min(ROUTE_TILE, t)
    nqd = PEER_HEADS * 2 * SUB_DIM
    scalar_shape = jax.ShapeDtypeStruct((PEER_HEADS, t // LANES, N_KEYS, LANES), jnp.int32)
    scalar_spec = pl.BlockSpec((PEER_HEADS, tile // LANES, N_KEYS, LANES), lambda i: (0, i, 0, 0))
    table_shape = jax.ShapeDtypeStruct((PEER_HEADS, N_KEYS, t), GATE_DTYPE)
    table_spec = pl.BlockSpec((PEER_HEADS, N_KEYS, tile), lambda i: (0, 0, i))
    return pl.pallas_call(
        _peer_route_kernel,
        grid=(t // tile,),
        in_specs=[
            pl.BlockSpec((tile, d), lambda i: (i, 0)),
            pl.BlockSpec((1, d), lambda i: (0, 0)),
            pl.BlockSpec((nqd, d), lambda i: (0, 0)),
            pl.BlockSpec((PEER_HEADS * 2, N_KEYS, SUB_DIM), lambda i: (0, 0, 0)),
        ],
        out_specs=[pl.BlockSpec((d, tile), lambda i: (0, i)), scalar_spec, scalar_spec, table_spec, table_spec],
        out_shape=[jax.ShapeDtypeStruct((d, t), BF16), scalar_shape, scalar_shape, table_shape, table_shape],
        scratch_shapes=[pltpu.VMEM((PEER_HEADS * 2, N_KEYS, tile), F32)],
        compiler_params=_params("parallel"),
        name="peer_route",
    )(x2, g, wq_t, keys)


def _plane_dot(lhs_ref, rows, rhs_ref, p, after=None):
    lhs = lhs_ref[p, rows, :]
    if after is not None:
        words = pltpu.bitcast(lhs, jnp.int32)
        reps = (words.shape[0] // after.shape[0], words.shape[1] // after.shape[1])
        lhs = pltpu.bitcast(words | jnp.tile(after, reps), lhs.dtype)
    return jnp.dot(lhs, rhs_ref[p * MXU_DEPTH:(p + 1) * MXU_DEPTH, :], preferred_element_type=F32)


def _k_planes(table):
    m, k = table.shape
    return table.reshape(m, k // MXU_DEPTH, MXU_DEPTH).transpose(1, 0, 2).astype(BF16)


def _k_planes_of_transpose(table):
    k, m = table.shape
    return jnp.swapaxes(table.reshape(k // MXU_DEPTH, MXU_DEPTH, m), 1, 2).astype(BF16)


def _zero_after(v):
    bits = lax.bitcast_convert_type(v, jnp.uint32)
    return lax.bitcast_convert_type((bits >> 16) >> 16, jnp.int32)


def _replicated_row(ref, lead, row):
    return ref[(*lead, pl.ds(row, SUBLANES, stride=0), slice(None))]


def _peer_dense_kernel(final_norm, n_chunks, xn_ref, x_ref, u_ref, vt_ref, r1_ref, c_ref, n2_ref, w_ref,
                       fg_ref, o_ref, acc_ref, h0_ref, h1_ref, a0_ref, a1_ref):
    s = pl.program_id(0)
    tile = xn_ref.shape[1]
    n_pieces = DENSE_PIECES
    up_rows = u_ref.shape[1] // n_pieces
    down_rows = vt_ref.shape[1] // n_pieces
    key_rows = N_KEYS // n_pieces
    rows_per_chunk = u_ref.shape[1] // N_KEYS
    slot = s % 2
    down_chunk = (s - 2) % n_chunks

    @pl.when(s == 0)
    def _():
        h1_ref[...] = jnp.zeros_like(h1_ref)
        a0_ref[...] = jnp.zeros_like(a0_ref)

    @pl.when((s < 2) | (down_chunk == 0))
    def _():
        acc_ref[...] = jnp.zeros_like(acc_ref)

    def run(h_up, h_gate, a_gate, a_down):
        def piece_body(k, carry):
            up = pl.ds(pl.multiple_of(k * up_rows, up_rows), up_rows)
            down = pl.ds(pl.multiple_of(k * down_rows, down_rows), down_rows)
            n_planes = u_ref.shape[0]
            slabs = [(j, sub) for j in range(rows_per_chunk) for sub in range(key_rows // SLAB_ROWS)]
            slabs_per_dot = len(slabs) // (2 * n_planes)
            slab = (SLAB_ROWS // BF16_ROWS, BF16_ROWS, tile)

            def scalar_rows(ref, hd, j, anchor):
                words = [_replicated_row(ref, (hd, lc), j) for lc in range(tile // LANES)]
                if anchor is not None:
                    words[-1] = words[-1] | anchor
                return jnp.concatenate([pltpu.bitcast(wd, GATE_DTYPE) for wd in words], axis=1)[None]

            def gate_slab(j, sub, anchor):
                key0 = pl.multiple_of(k * key_rows + sub * SLAB_ROWS, SLAB_ROWS)
                keys = pl.ds(key0, SLAB_ROWS)
                rows = pl.ds(pl.multiple_of(j * N_KEYS + key0, SLAB_ROWS), SLAB_ROWS)
                h = h_gate[rows, :]
                act = (0.5 * h * (1.0 + lax.erf(h * (1.0 / math.sqrt(2.0))))).astype(GATE_DTYPE)
                gate = jnp.zeros(slab, GATE_DTYPE)
                for hd in range(PEER_HEADS):
                    r1 = scalar_rows(r1_ref, hd, j, anchor if hd == PEER_HEADS - 1 else None)
                    c = scalar_rows(c_ref, hd, j, None)
                    n2 = n2_ref[hd, keys, :].reshape(slab)
                    w = w_ref[hd, keys, :].reshape(slab)
                    gate = gate + jnp.where(r1 < n2, w, jnp.zeros_like(w)) * c
                out = (gate.reshape(SLAB_ROWS, tile) * act).astype(a_gate.dtype)
                a_gate[rows, :] = out
                return _zero_after(pltpu.bitcast(out[:, :LANES], jnp.int32))

            dots = ([("down", vt_ref, down, a_down, p) for p in range(n_planes)]
                    + [("up", u_ref, up, xn_ref, p) for p in range(n_planes)])
            sums = {}
            tokens = []
            for i, (name, lhs_ref, rows, rhs_ref, p) in enumerate(dots):
                after = tokens[i - DOT_LAG] if i >= DOT_LAG else None
                part = _plane_dot(lhs_ref, rows, rhs_ref, p, after)
                sums[name] = part if p == 0 else sums[name] + part
                if (name, p) == ("up", n_planes - 1):
                    h_up[up, :] = sums["up"]
                if (name, p) == ("down", n_planes - 1):
                    acc_ref[down, :] += sums["down"]
                for j, sub in slabs[i * slabs_per_dot:(i + 1) * slabs_per_dot]:
                    anchor = None
                    if (j, sub) == slabs[-1]:
                        anchor = (_zero_after(sums["up"][-SUBLANES:, :LANES])
                                  | _zero_after(sums["up"][-SUBLANES:, -LANES:])
                                  | _zero_after(sums["down"][-SUBLANES:, :LANES])
                                  | _zero_after(sums["down"][-SUBLANES:, -LANES:]))
                    token = gate_slab(j, sub, anchor)
                tokens.append(token)
            return carry

        for piece in range(n_pieces):
            piece_body(piece, 0)

    @pl.when(slot == 0)
    def _():
        run(h0_ref, h1_ref, a1_ref, a0_ref)

    @pl.when(slot == 1)
    def _():
        run(h1_ref, h0_ref, a0_ref, a1_ref)

    @pl.when((s >= 2) & (down_chunk == n_chunks - 1))
    def _():
        y = x_ref[...] + acc_ref[...].T
        if final_norm:
            y = _rmsnorm(y, fg_ref[...])
        o_ref[...] = y


def _peer_dense(xn, x2, u, vt, r1, c, n2, w, final_g, final_norm):
    t, d = x2.shape
    n_experts = u.shape[1]
    tile = min(DENSE_TILE, t)
    ec = EXPERT_CHUNK
    rows = ec // N_KEYS
    n_chunks = n_experts // ec
    n_pos = (t // tile) * n_chunks

    def pos(s, lag):
        p = jnp.clip(s - lag, 0, n_pos - 1)
        return p // n_chunks, p % n_chunks

    up_tile = lambda s: pos(s, 0)[0]
    up_chunk = lambda s: pos(s, 0)[1]
    gate_tile = lambda s: pos(s, 1)[0]
    gate_chunk = lambda s: pos(s, 1)[1]
    down_tile = lambda s: pos(s, 2)[0]
    down_chunk = lambda s: pos(s, 2)[1]
    return pl.pallas_call(
        functools.partial(_peer_dense_kernel, final_norm, n_chunks),
        grid=(n_pos + 2,),
        in_specs=[
            pl.BlockSpec((d, tile), lambda s: (0, up_tile(s))),
            pl.BlockSpec((tile, d), lambda s: (down_tile(s), 0)),
            pl.BlockSpec((d // MXU_DEPTH, ec, MXU_DEPTH), lambda s: (0, up_chunk(s), 0)),
            pl.BlockSpec((ec // MXU_DEPTH, d, MXU_DEPTH), lambda s: (down_chunk(s), 0, 0)),
            pl.BlockSpec((PEER_HEADS, tile // LANES, rows, LANES), lambda s: (0, gate_tile(s), gate_chunk(s), 0)),
            pl.BlockSpec((PEER_HEADS, tile // LANES, rows, LANES), lambda s: (0, gate_tile(s), gate_chunk(s), 0)),
            pl.BlockSpec((PEER_HEADS, N_KEYS, tile), lambda s: (0, 0, gate_tile(s))),
            pl.BlockSpec((PEER_HEADS, N_KEYS, tile), lambda s: (0, 0, gate_tile(s))),
            pl.BlockSpec((1, d), lambda s: (0, 0)),
        ],
        out_specs=pl.BlockSpec((tile, d), lambda s: (down_tile(s), 0)),
        out_shape=jax.ShapeDtypeStruct((t, d), F32),
        scratch_shapes=[
            pltpu.VMEM((d, tile), F32),
            pltpu.VMEM((ec, tile), F32),
            pltpu.VMEM((ec, tile), F32),
            pltpu.VMEM((ec, tile), BF16),
            pltpu.VMEM((ec, tile), BF16),
        ],
        compiler_params=_params("arbitrary"),
        name="peer_dense",
    )(xn, x2, u, vt, r1, c, n2, w, final_g)


def _peer(x2, g, w_q, subkeys, u_tab, v_tab, final_g, final_norm):
    d = x2.shape[1]
    wq_t = w_q.T.astype(BF16)
    keys = subkeys.reshape(PEER_HEADS * 2, N_KEYS, SUB_DIM).astype(BF16)
    xn, r1, c, n2, w = _peer_route(x2, g.reshape(1, d), wq_t, keys)
    return _peer_dense(xn, x2, _k_planes(u_tab), _k_planes_of_transpose(v_tab), r1, c, n2, w,
                       final_g.reshape(1, d), final_norm)


def kernel(x, conv_norm_g, conv_w_in, conv_w, conv_w_out, attn_norm_g, attn_w_qkv, attn_sink, attn_w_o,
           rel_bias, ffn_norm_g, peer_w_q, peer_subkeys, peer_u, peer_v, final_norm_g):
    bsz, seq, d = x.shape
    depth = ffn_norm_g.shape[0]
    x2 = x.reshape(bsz * seq, d)
    bucket = _t5_bucket_table()
    for i in range(depth):
        j = i // 2
        if i % 2 == 0:
            u, gb = _conv_in(x2, conv_norm_g[j].reshape(1, d), conv_w_in[j].astype(BF16))
            x2 = _conv_out(u, gb, x2, conv_w[j], conv_w_out[j].astype(BF16), seq)
        else:
            q, k, v = _attn_qkv(x2, attn_norm_g[j].reshape(1, d), attn_w_qkv[j].astype(BF16))
            x2 = _attn(q, k, v, bucket, rel_bias, attn_sink[j], x2, attn_w_o[j].astype(BF16), seq)
        x2 = _peer(x2, ffn_norm_g[i], peer_w_q[i], peer_subkeys[i], peer_u[i], peer_v[i],
                   final_norm_g, i == depth - 1)
    return x2.reshape(bsz, seq, d)
```

```python
import functools
import math

import jax
import jax.numpy as jnp
import numpy as np
from jax import lax
from jax.experimental import pallas as pl
from jax.experimental.pallas import tpu as pltpu

F32 = jnp.float32
BF16 = jnp.bfloat16
GATE_DTYPE = jnp.bfloat16

RMS_EPS = 1e-6
CONV_WIDTH = 3
N_HEADS = 16
N_KV_HEADS = 4
HEAD_DIM = 64
GROUP = N_HEADS // N_KV_HEADS
WINDOW = 128
BLOCK = 128
NEG_INF = -1e30
N_BUCKETS = 32
MAX_DISTANCE = 128
PEER_HEADS = 8
N_KEYS = 128
PEER_TOPK = 16
SUB_DIM = 128

SUBLANES = 8
LANES = 128
BF16_ROWS = 2 * SUBLANES
MXU_DEPTH = 256
VMEM_LIMIT_BYTES = 52 * 1024 * 1024

CONV_TILE = 512
QKV_TILE = 512
ROUTE_TILE = 256
ROUTE_HEADS_PER_ITER = 4
DENSE_TILE = 512
EXPERT_CHUNK = 1024
DENSE_PIECES = 2
DOT_LAG = 1
SLAB_ROWS = 32
SLAB_I1 = 2


def _params(*semantics, flags=None):
    return pltpu.CompilerParams(dimension_semantics=semantics, vmem_limit_bytes=VMEM_LIMIT_BYTES,
                                flags=flags)


def _rmsnorm(x, g):
    ms = jnp.mean(x * x, axis=-1, keepdims=True)
    return x * lax.rsqrt(ms + RMS_EPS) * g


def _conv_in_kernel(x_ref, g_ref, w_ref, u_ref, gb_ref):
    d = x_ref.shape[1]
    xn = _rmsnorm(x_ref[...], g_ref[...]).astype(BF16)
    gb_ref[...] = jnp.dot(xn, w_ref[:, :d], preferred_element_type=F32)
    gate_c = jnp.dot(xn, w_ref[:, d:2 * d], preferred_element_type=F32)
    h = jnp.dot(xn, w_ref[:, 2 * d:], preferred_element_type=F32)
    u_ref[...] = gate_c * h


def _conv_in(x2, g, w_in):
    t, d = x2.shape
    tile = min(CONV_TILE, t)
    return pl.pallas_call(
        _conv_in_kernel,
        grid=(t // tile,),
        in_specs=[
            pl.BlockSpec((tile, d), lambda i: (i, 0)),
            pl.BlockSpec((1, d), lambda i: (0, 0)),
            pl.BlockSpec((d, 3 * d), lambda i: (0, 0)),
        ],
        out_specs=[
            pl.BlockSpec((tile, d), lambda i: (i, 0)),
            pl.BlockSpec((tile, d), lambda i: (i, 0)),
        ],
        out_shape=[jax.ShapeDtypeStruct((t, d), F32), jax.ShapeDtypeStruct((t, d), F32)],
        compiler_params=_params("parallel"),
        name="conv_in",
    )(x2, g, w_in)


def _conv_out_kernel(tiles_per_seq, u_ref, uprev_ref, unext_ref, gb_ref, x_ref, cw_ref, w_ref, o_ref):
    tile = u_ref.shape[0]
    pos = pl.program_id(0) % tiles_per_seq
    u = u_ref[...]
    prev_row = jnp.where(pos == 0, 0.0, uprev_ref[SUBLANES - 1:SUBLANES, :])
    next_row = jnp.where(pos == tiles_per_seq - 1, 0.0, unext_ref[0:1, :])
    row = lax.broadcasted_iota(jnp.int32, u.shape, 0)
    u_before = jnp.where(row == 0, prev_row, pltpu.roll(u, 1, axis=0))
    u_after = jnp.where(row == tile - 1, next_row, pltpu.roll(u, tile - 1, axis=0))
    y = cw_ref[0:1, :] * u_before + cw_ref[1:2, :] * u + cw_ref[2:3, :] * u_after
    z = (gb_ref[...] * y).astype(BF16)
    o_ref[...] = x_ref[...] + jnp.dot(z, w_ref[...], preferred_element_type=F32)


def _conv_out(u, gb, x2, conv_w, w_out, seq):
    t, d = x2.shape
    tile = min(CONV_TILE, seq)
    halo_per_tile = tile // SUBLANES
    n_halo = t // SUBLANES
    return pl.pallas_call(
        functools.partial(_conv_out_kernel, seq // tile),
        grid=(t // tile,),
        in_specs=[
            pl.BlockSpec((tile, d), lambda i: (i, 0)),
            pl.BlockSpec((SUBLANES, d), lambda i: (jnp.maximum(i * halo_per_tile - 1, 0), 0)),
            pl.BlockSpec((SUBLANES, d), lambda i: (jnp.minimum((i + 1) * halo_per_tile, n_halo - 1), 0)),
            pl.BlockSpec((tile, d), lambda i: (i, 0)),
            pl.BlockSpec((tile, d), lambda i: (i, 0)),
            pl.BlockSpec((CONV_WIDTH, d), lambda i: (0, 0)),
            pl.BlockSpec((d, d), lambda i: (0, 0)),
        ],
        out_specs=pl.BlockSpec((tile, d), lambda i: (i, 0)),
        out_shape=jax.ShapeDtypeStruct((t, d), F32),
        compiler_params=_params("parallel"),
        name="conv_out",
    )(u, u, u, gb, x2, conv_w, w_out)


def _attn_qkv_kernel(x_ref, g_ref, w_ref, q_ref, k_ref, v_ref):
    nq = q_ref.shape[1]
    nk = k_ref.shape[1]
    xn = _rmsnorm(x_ref[...], g_ref[...]).astype(BF16)
    q = jnp.dot(xn, w_ref[:, :nq], preferred_element_type=F32) * (1.0 / math.sqrt(HEAD_DIM))
    q_ref[...] = q.astype(BF16)
    k_ref[...] = jnp.dot(xn, w_ref[:, nq:nq + nk], preferred_element_type=F32).astype(BF16)
    v_ref[...] = jnp.dot(xn, w_ref[:, nq + nk:], preferred_element_type=F32).astype(BF16)


def _attn_qkv(x2, g, w_qkv):
    t, d = x2.shape
    tile = min(QKV_TILE, t)
    nq = N_HEADS * HEAD_DIM
    nk = N_KV_HEADS * HEAD_DIM
    return pl.pallas_call(
        _attn_qkv_kernel,
        grid=(t // tile,),
        in_specs=[
            pl.BlockSpec((tile, d), lambda i: (i, 0)),
            pl.BlockSpec((1, d), lambda i: (0, 0)),
            pl.BlockSpec((d, nq + 2 * nk), lambda i: (0, 0)),
        ],
        out_specs=[
            pl.BlockSpec((tile, nq), lambda i: (i, 0)),
            pl.BlockSpec((tile, nk), lambda i: (i, 0)),
            pl.BlockSpec((tile, nk), lambda i: (i, 0)),
        ],
        out_shape=[
            jax.ShapeDtypeStruct((t, nq), BF16),
            jax.ShapeDtypeStruct((t, nk), BF16),
            jax.ShapeDtypeStruct((t, nk), BF16),
        ],
        compiler_params=_params("parallel"),
        name="attn_qkv",
    )(x2, g, w_qkv)


def _attn_kernel(blocks_per_seq, q_ref, kp_ref, kc_ref, kn_ref, vp_ref, vc_ref, vn_ref, bucket_ref,
                 relb_ref, sink_ref, x_ref, wo_ref, o_ref, bias_ref, att_ref):
    step = pl.program_id(0)

    @pl.when(step == 0)
    def _():
        bucket = bucket_ref[...]
        rel = lax.broadcasted_iota(jnp.int32, bucket.shape, 1) - BLOCK - lax.broadcasted_iota(
            jnp.int32, bucket.shape, 0)
        in_window = jnp.abs(rel) <= WINDOW

        def head_body(hq, carry):
            def bucket_body(b, acc):
                return jnp.where(bucket == b, relb_ref[b, hq], acc)

            table = lax.fori_loop(0, N_BUCKETS, bucket_body, jnp.zeros(bucket.shape, F32))
            bias_ref[hq] = jnp.where(in_window, table, NEG_INF)
            return carry

        lax.fori_loop(0, N_HEADS, head_body, 0)

    pos = step % blocks_per_seq
    has_prev = pos > 0
    has_next = pos < blocks_per_seq - 1
    col = lax.broadcasted_iota(jnp.int32, (BLOCK, 3 * BLOCK), 1)
    key_ok = ((col >= BLOCK) | has_prev) & ((col < 2 * BLOCK) | has_next)

    for kvh in range(N_KV_HEADS):
        lo = kvh * HEAD_DIM
        k = jnp.concatenate([kp_ref[:, lo:lo + HEAD_DIM], kc_ref[:, lo:lo + HEAD_DIM],
                             kn_ref[:, lo:lo + HEAD_DIM]], axis=0)
        v = jnp.concatenate([vp_ref[:, lo:lo + HEAD_DIM], vc_ref[:, lo:lo + HEAD_DIM],
                             vn_ref[:, lo:lo + HEAD_DIM]], axis=0)
        for g in range(GROUP):
            hq = kvh * GROUP + g
            q = q_ref[:, hq * HEAD_DIM:(hq + 1) * HEAD_DIM]
            s = lax.dot_general(q, k, (((1,), (1,)), ((), ())), preferred_element_type=F32)
            logits = jnp.where(key_ok, s + bias_ref[hq], NEG_INF)
            sink = sink_ref[hq]
            m = jnp.maximum(jnp.max(logits, axis=-1, keepdims=True), sink)
            p = jnp.exp(logits - m)
            den = jnp.sum(p, axis=-1, keepdims=True) + jnp.exp(sink - m)
            o = jnp.dot(p.astype(BF16), v, preferred_element_type=F32) / den
            att_ref[:, hq * HEAD_DIM:(hq + 1) * HEAD_DIM] = o.astype(BF16)

    o_ref[...] = x_ref[...] + jnp.dot(att_ref[...], wo_ref[...], preferred_element_type=F32)


def _attn(q, k, v, bucket, rel_bias, sink, x2, w_o, seq):
    t, d = x2.shape
    nq = N_HEADS * HEAD_DIM
    nk = N_KV_HEADS * HEAD_DIM
    n_blocks = t // BLOCK
    prev_map = lambda i: (jnp.maximum(i - 1, 0), 0)
    cur_map = lambda i: (i, 0)
    next_map = lambda i: (jnp.minimum(i + 1, n_blocks - 1), 0)
    smem = pl.BlockSpec(memory_space=pltpu.SMEM)
    return pl.pallas_call(
        functools.partial(_attn_kernel, seq // BLOCK),
        grid=(n_blocks,),
        in_specs=[
            pl.BlockSpec((BLOCK, nq), cur_map),
            pl.BlockSpec((BLOCK, nk), prev_map),
            pl.BlockSpec((BLOCK, nk), cur_map),
            pl.BlockSpec((BLOCK, nk), next_map),
            pl.BlockSpec((BLOCK, nk), prev_map),
            pl.BlockSpec((BLOCK, nk), cur_map),
            pl.BlockSpec((BLOCK, nk), next_map),
            pl.BlockSpec((BLOCK, 3 * BLOCK), lambda i: (0, 0)),
            smem,
            smem,
            pl.BlockSpec((BLOCK, d), cur_map),
            pl.BlockSpec((nq, d), lambda i: (0, 0)),
        ],
        out_specs=pl.BlockSpec((BLOCK, d), cur_map),
        out_shape=jax.ShapeDtypeStruct((t, d), F32),
        scratch_shapes=[
            pltpu.VMEM((N_HEADS, BLOCK, 3 * BLOCK), F32),
            pltpu.VMEM((BLOCK, nq), BF16),
        ],
        compiler_params=_params("arbitrary"),
        name="attn",
    )(q, k, k, k, v, v, v, bucket, rel_bias, sink, x2, w_o)


def _t5_bucket_table():
    half = N_BUCKETS // 2
    max_exact = half // 2
    qi = jnp.arange(BLOCK)[:, None]
    kj = jnp.arange(3 * BLOCK)[None, :]
    rel = kj - BLOCK - qi
    ret = jnp.where(rel > 0, half, 0)
    n = jnp.abs(rel)
    nf = jnp.maximum(n, 1).astype(jnp.float32)
    large = max_exact + (jnp.log(nf / max_exact) / math.log(MAX_DISTANCE / max_exact)
                         * (half - max_exact)).astype(jnp.int32)
    large = jnp.minimum(large, half - 1)
    return (ret + jnp.where(n < max_exact, n, large)).astype(jnp.int32)


INT32_MIN = -(2 ** 31)


def _ordered_int(bits):
    return bits ^ ((bits >> 31) & 0x7FFFFFFF)


def _top16_ranks(s, key_iota):
    bits = lax.bitcast_convert_type(s, jnp.int32)
    keys = _ordered_int(jnp.where(bits == INT32_MIN, 0, bits))
    slot = lax.broadcasted_iota(jnp.int32, (PEER_TOPK, s.shape[1]), 0)
    top = jnp.zeros((PEER_TOPK, s.shape[1]), jnp.int32)
    for r in range(PEER_TOPK):
        m = jnp.max(keys, axis=0, keepdims=True)
        first = jnp.min(jnp.where(keys == m, key_iota, N_KEYS), axis=0, keepdims=True)
        keys = jnp.where(key_iota == first, INT32_MIN + r, keys)
        top = jnp.where(slot == r, m, top)
    rank = jnp.where(keys < INT32_MIN + PEER_TOPK, keys ^ INT32_MIN, PEER_TOPK).astype(F32)
    return rank, lax.bitcast_convert_type(_ordered_int(top), F32)


def _staircase(sv1, sv2):
    lanes = sv1.shape[1]
    slot = lax.broadcasted_iota(jnp.int32, (PEER_TOPK, lanes), 0)
    n = jnp.zeros((PEER_TOPK, lanes), jnp.int32)
    front = sv1[0:1, :] + sv2
    big = PEER_TOPK * PEER_TOPK + PEER_TOPK
    for r in range(PEER_TOPK):
        m = jnp.max(front, axis=0, keepdims=True)
        flat = n * PEER_TOPK + slot
        first = jnp.min(jnp.where(front == m, flat, big), axis=0, keepdims=True)
        chosen = flat == first
        n = n + chosen.astype(jnp.int32)
        if r + 1 < PEER_TOPK:
            depth = jnp.sum(jnp.where(chosen, n, 0), axis=0, keepdims=True)
            nxt = jnp.sum(jnp.where(slot == depth, sv1, 0.0), axis=0, keepdims=True)
            front = jnp.where(chosen, jnp.where(depth < PEER_TOPK, nxt + sv2, -jnp.inf), front)
    return n.astype(F32)


def _twice_bf16(v):
    bits = lax.bitcast_convert_type(v.astype(GATE_DTYPE).astype(F32), jnp.uint32)
    return lax.bitcast_convert_type(bits | (bits >> 16), jnp.int32)


def _peer_route_kernel(x_ref, g_ref, wq_ref, keys_ref, xn_ref, r1_ref, c_ref, n2_ref, w_ref, s_ref):
    tile = x_ref.shape[0]
    xn_t = _rmsnorm(x_ref[...], g_ref[...]).T.astype(BF16)
    xn_ref[...] = xn_t
    q_t = jnp.dot(wq_ref[...], xn_t, preferred_element_type=F32)
    for hc in range(PEER_HEADS * 2):
        q_hc = q_t[hc * SUB_DIM:(hc + 1) * SUB_DIM, :].astype(BF16)
        s_ref[hc] = jnp.dot(keys_ref[hc], q_hc, preferred_element_type=F32)

    key_iota = lax.broadcasted_iota(jnp.int32, (N_KEYS, LANES), 0)
    slot = lax.broadcasted_iota(jnp.int32, (PEER_TOPK, LANES), 0).astype(F32)

    def route_head(hd, lane_chunk):
        lane0 = pl.multiple_of(lane_chunk * LANES, LANES)
        lanes = pl.ds(lane0, LANES)
        s1 = s_ref[2 * hd, :, lanes]
        s2 = s_ref[2 * hd + 1, :, lanes]
        rank1, sv1 = _top16_ranks(s1, key_iota)
        rank2, sv2 = _top16_ranks(s2, key_iota)
        n = _staircase(sv1, sv2)
        e1 = jnp.exp(s1 - sv1[0:1, :])
        e2 = jnp.exp(s2 - sv2[0:1, :])
        e1s = jnp.exp(sv1 - sv1[0:1, :])
        e2s = jnp.exp(sv2 - sv2[0:1, :])
        inner = jnp.zeros_like(e2s)
        for p in range(PEER_TOPK):
            inner = inner + jnp.where(n > float(p), e1s[p:p + 1, :], 0.0)
        z = jnp.sum(inner * e2s, axis=0, keepdims=True)
        n2 = jnp.zeros_like(rank2)
        for q in range(PEER_TOPK):
            n2 = jnp.where(rank2 == float(q), n[q:q + 1, :], n2)
        r1_ref[hd, lane_chunk] = _twice_bf16(rank1)
        c_ref[hd, lane_chunk] = _twice_bf16(e1 / z)
        n2_ref[hd, :, lanes] = n2.astype(GATE_DTYPE)
        w_ref[hd, :, lanes] = e2.astype(GATE_DTYPE)

    def body(it, carry):
        lane_chunk = it % (tile // LANES)
        for i in range(ROUTE_HEADS_PER_ITER):
            route_head((it // (tile // LANES)) * ROUTE_HEADS_PER_ITER + i, lane_chunk)
        return carry

    lax.fori_loop(0, (PEER_HEADS // ROUTE_HEADS_PER_ITER) * (tile // LANES), body, 0)


def _peer_route(x2, g, wq_t, keys):
    t, d = x2.shape
    tile = min(ROUTE_TILE, t)
    nqd = PEER_HEADS * 2 * SUB_DIM
    scalar_shape = jax.ShapeDtypeStruct((PEER_HEADS, t // LANES, N_KEYS, LANES), jnp.int32)
    scalar_spec = pl.BlockSpec((PEER_HEADS, tile // LANES, N_KEYS, LANES), lambda i: (0, i, 0, 0))
    table_shape = jax.ShapeDtypeStruct((PEER_HEADS, N_KEYS, t), GATE_DTYPE)
    table_spec = pl.BlockSpec((PEER_HEADS, N_KEYS, tile), lambda i: (0, 0, i))
    return pl.pallas_call(
        _peer_route_kernel,
        grid=(t // tile,),
        in_specs=[
            pl.BlockSpec((tile, d), lambda i: (i, 0)),
            pl.BlockSpec((1, d), lambda i: (0, 0)),
            pl.BlockSpec((nqd, d), lambda i: (0, 0)),
            pl.BlockSpec((PEER_HEADS * 2, N_KEYS, SUB_DIM), lambda i: (0, 0, 0)),
        ],
        out_specs=[pl.BlockSpec((d, tile), lambda i: (0, i)), scalar_spec, scalar_spec, table_spec, table_spec],
        out_shape=[jax.ShapeDtypeStruct((d, t), BF16), scalar_shape, scalar_shape, table_shape, table_shape],
        scratch_shapes=[pltpu.VMEM((PEER_HEADS * 2, N_KEYS, tile), F32)],
        compiler_params=_params("parallel"),
        name="peer_route",
    )(x2, g, wq_t, keys)


def _plane_dot(lhs_ref, rows, rhs_ref, p, after=None):
    lhs = lhs_ref[p, rows, :]
    if after is not None:
        words = pltpu.bitcast(lhs, jnp.int32)
        reps = (words.shape[0] // after.shape[0], words.shape[1] // after.shape[1])
        lhs = pltpu.bitcast(words | jnp.tile(after, reps), lhs.dtype)
    return jnp.dot(lhs, rhs_ref[p * MXU_DEPTH:(p + 1) * MXU_DEPTH, :], preferred_element_type=F32)


def _k_planes(table):
    m, k = table.shape
    return table.reshape(m, k // MXU_DEPTH, MXU_DEPTH).transpose(1, 0, 2).astype(BF16)


def _k_planes_of_transpose(table):
    k, m = table.shape
    return jnp.swapaxes(table.reshape(k // MXU_DEPTH, MXU_DEPTH, m), 1, 2).astype(BF16)


def _zero_after(v):
    bits = lax.bitcast_convert_type(v, jnp.uint32)
    return lax.bitcast_convert_type((bits >> 16) >> 16, jnp.int32)


def _replicated_row(ref, lead, row):
    return ref[(*lead, pl.ds(row, SUBLANES, stride=0), slice(None))]


def _peer_dense_kernel(final_norm, n_chunks, xn_ref, x_ref, u_ref, vt_ref, r1_ref, c_ref, n2_ref, w_ref,
                       fg_ref, o_ref, acc_ref, h0_ref, h1_ref, a0_ref, a1_ref):
    s = pl.program_id(0)
    tile = xn_ref.shape[1]
    n_pieces = DENSE_PIECES
    up_rows = u_ref.shape[1] // n_pieces
    down_rows = vt_ref.shape[1] // n_pieces
    key_rows = N_KEYS // n_pieces
    rows_per_chunk = u_ref.shape[1] // N_KEYS
    slot = s % 2
    down_chunk = (s - 2) % n_chunks

    @pl.when(s == 0)
    def _():
        h1_ref[...] = jnp.zeros_like(h1_ref)
        a0_ref[...] = jnp.zeros_like(a0_ref)

    @pl.when((s < 2) | (down_chunk == 0))
    def _():
        acc_ref[...] = jnp.zeros_like(acc_ref)

    def run(h_up, h_gate, a_gate, a_down):
        def piece_body(k, carry):
            up = pl.ds(pl.multiple_of(k * up_rows, up_rows), up_rows)
            down = pl.ds(pl.multiple_of(k * down_rows, down_rows), down_rows)
            n_planes = u_ref.shape[0]
            slabs = [(jg, sub) for jg in range(rows_per_chunk // SLAB_I1) for sub in range(key_rows // SLAB_ROWS)]
            slabs_per_dot = len(slabs) // (2 * n_planes)
            slab = (SLAB_ROWS // BF16_ROWS, BF16_ROWS, tile)

            def scalar_rows(ref, hd, j, anchor):
                words = [_replicated_row(ref, (hd, lc), j) for lc in range(tile // LANES)]
                if anchor is not None:
                    words[-1] = words[-1] | anchor
                return jnp.concatenate([pltpu.bitcast(wd, GATE_DTYPE) for wd in words], axis=1)[None]

            def gate_slab(jg, sub, anchor):
                key0 = pl.multiple_of(k * key_rows + sub * SLAB_ROWS, SLAB_ROWS)
                keys = pl.ds(key0, SLAB_ROWS)
                i1_rows = range(jg * SLAB_I1, (jg + 1) * SLAB_I1)
                gates = [jnp.zeros(slab, GATE_DTYPE) for _ in i1_rows]
                for hd in range(PEER_HEADS):
                    n2 = n2_ref[hd, keys, :].reshape(slab)
                    w = w_ref[hd, keys, :].reshape(slab)
                    for idx, j in enumerate(i1_rows):
                        hold = anchor if (hd == PEER_HEADS - 1 and idx == SLAB_I1 - 1) else None
                        r1 = scalar_rows(r1_ref, hd, j, hold)
                        c = scalar_rows(c_ref, hd, j, None)
                        gates[idx] = gates[idx] + jnp.where(r1 < n2, w, jnp.zeros_like(w)) * c
                for idx, j in enumerate(i1_rows):
                    rows = pl.ds(pl.multiple_of(j * N_KEYS + key0, SLAB_ROWS), SLAB_ROWS)
                    h = h_gate[rows, :]
                    act = (0.5 * h * (1.0 + lax.erf(h * (1.0 / math.sqrt(2.0))))).astype(GATE_DTYPE)
                    out = (gates[idx].reshape(SLAB_ROWS, tile) * act).astype(a_gate.dtype)
                    a_gate[rows, :] = out
                return _zero_after(pltpu.bitcast(out[:, :LANES], jnp.int32))

            dots = ([("down", vt_ref, down, a_down, p) for p in range(n_planes)]
                    + [("up", u_ref, up, xn_ref, p) for p in range(n_planes)])
            sums = {}
            tokens = []
            for i, (name, lhs_ref, rows, rhs_ref, p) in enumerate(dots):
                after = tokens[i - DOT_LAG] if i >= DOT_LAG else None
                part = _plane_dot(lhs_ref, rows, rhs_ref, p, after)
                sums[name] = part if p == 0 else sums[name] + part
                if (name, p) == ("up", n_planes - 1):
                    h_up[up, :] = sums["up"]
                if (name, p) == ("down", n_planes - 1):
                    acc_ref[down, :] += sums["down"]
                for j, sub in slabs[i * slabs_per_dot:(i + 1) * slabs_per_dot]:
                    anchor = None
                    if (j, sub) == slabs[-1]:
                        anchor = (_zero_after(sums["up"][-SUBLANES:, :LANES])
                                  | _zero_after(sums["up"][-SUBLANES:, -LANES:])
                                  | _zero_after(sums["down"][-SUBLANES:, :LANES])
                                  | _zero_after(sums["down"][-SUBLANES:, -LANES:]))
                    token = gate_slab(j, sub, anchor)
                tokens.append(token)
            return carry

        lax.fori_loop(0, n_pieces, piece_body, 0)

    @pl.when(slot == 0)
    def _():
        run(h0_ref, h1_ref, a1_ref, a0_ref)

    @pl.when(slot == 1)
    def _():
        run(h1_ref, h0_ref, a0_ref, a1_ref)

    @pl.when((s >= 2) & (down_chunk == n_chunks - 1))
    def _():
        y = x_ref[...] + acc_ref[...].T
        if final_norm:
            y = _rmsnorm(y, fg_ref[...])
        o_ref[...] = y


def _peer_dense(xn, x2, u, vt, r1, c, n2, w, final_g, final_norm):
    t, d = x2.shape
    n_experts = u.shape[1]
    tile = min(DENSE_TILE, t)
    ec = EXPERT_CHUNK
    rows = ec // N_KEYS
    n_chunks = n_experts // ec
    n_pos = (t // tile) * n_chunks

    def pos(s, lag):
        p = jnp.clip(s - lag, 0, n_pos - 1)
        return p // n_chunks, p % n_chunks

    up_tile = lambda s: pos(s, 0)[0]
    up_chunk = lambda s: pos(s, 0)[1]
    gate_tile = lambda s: pos(s, 1)[0]
    gate_chunk = lambda s: pos(s, 1)[1]
    down_tile = lambda s: pos(s, 2)[0]
    down_chunk = lambda s: pos(s, 2)[1]
    return pl.pallas_call(
        functools.partial(_peer_dense_kernel, final_norm, n_chunks),
        grid=(n_pos + 2,),
        in_specs=[
            pl.BlockSpec((d, tile), lambda s: (0, up_tile(s))),
            pl.BlockSpec((tile, d), lambda s: (down_tile(s), 0)),
            pl.BlockSpec((d // MXU_DEPTH, ec, MXU_DEPTH), lambda s: (0, up_chunk(s), 0)),
            pl.BlockSpec((ec // MXU_DEPTH, d, MXU_DEPTH), lambda s: (down_chunk(s), 0, 0)),
            pl.BlockSpec((PEER_HEADS, tile // LANES, rows, LANES), lambda s: (0, gate_tile(s), gate_chunk(s), 0)),
            pl.BlockSpec((PEER_HEADS, tile // LANES, rows, LANES), lambda s: (0, gate_tile(s), gate_chunk(s), 0)),
            pl.BlockSpec((PEER_HEADS, N_KEYS, tile), lambda s: (0, 0, gate_tile(s))),
            pl.BlockSpec((PEER_HEADS, N_KEYS, tile), lambda s: (0, 0, gate_tile(s))),
            pl.BlockSpec((1, d), lambda s: (0, 0)),
        ],
        out_specs=pl.BlockSpec((tile, d), lambda s: (down_tile(s), 0)),
        out_shape=jax.ShapeDtypeStruct((t, d), F32),
        scratch_shapes=[
            pltpu.VMEM((d, tile), F32),
            pltpu.VMEM((ec, tile), F32),
            pltpu.VMEM((ec, tile), F32),
            pltpu.VMEM((ec, tile), BF16),
            pltpu.VMEM((ec, tile), BF16),
        ],
        compiler_params=_params("arbitrary"),
        name="peer_dense",
    )(xn, x2, u, vt, r1, c, n2, w, final_g)


def _peer(x2, g, w_q, subkeys, u_tab, v_tab, final_g, final_norm):
    d = x2.shape[1]
    wq_t = w_q.T.astype(BF16)
    keys = subkeys.reshape(PEER_HEADS * 2, N_KEYS, SUB_DIM).astype(BF16)
    xn, r1, c, n2, w = _peer_route(x2, g.reshape(1, d), wq_t, keys)
    return _peer_dense(xn, x2, _k_planes(u_tab), _k_planes_of_transpose(v_tab), r1, c, n2, w,
                       final_g.reshape(1, d), final_norm)


def kernel(x, conv_norm_g, conv_w_in, conv_w, conv_w_out, attn_norm_g, attn_w_qkv, attn_sink, attn_w_o,
           rel_bias, ffn_norm_g, peer_w_q, peer_subkeys, peer_u, peer_v, final_norm_g):
    bsz, seq, d = x.shape
    depth = ffn_norm_g.shape[0]
    x2 = x.reshape(bsz * seq, d)
    bucket = _t5_bucket_table()
    for i in range(depth):
        j = i // 2
        if i % 2 == 0:
            u, gb = _conv_in(x2, conv_norm_g[j].reshape(1, d), conv_w_in[j].astype(BF16))
            x2 = _conv_out(u, gb, x2, conv_w[j], conv_w_out[j].astype(BF16), seq)
        else:
            q, k, v = _attn_qkv(x2, attn_norm_g[j].reshape(1, d), attn_w_qkv[j].astype(BF16))
            x2 = _attn(q, k, v, bucket, rel_bias, attn_sink[j], x2, attn_w_o[j].astype(BF16), seq)
        x2 = _peer(x2, ffn_norm_g[i], peer_w_q[i], peer_subkeys[i], peer_u[i], peer_v[i],
                   final_norm_g, i == depth - 1)
    return x2.reshape(bsz, seq, d)
```

```python
import functools
import math

import jax
import jax.numpy as jnp
import numpy as np
from jax import lax
from jax.experimental import pallas as pl
from jax.experimental.pallas import tpu as pltpu

F32 = jnp.float32
BF16 = jnp.bfloat16
GATE_DTYPE = jnp.bfloat16

RMS_EPS = 1e-6
CONV_WIDTH = 3
N_HEADS = 16
N_KV_HEADS = 4
HEAD_DIM = 64
GROUP = N_HEADS // N_KV_HEADS
WINDOW = 128
BLOCK = 128
NEG_INF = -1e30
N_BUCKETS = 32
MAX_DISTANCE = 128
PEER_HEADS = 8
N_KEYS = 128
PEER_TOPK = 16
SUB_DIM = 128

SUBLANES = 8
LANES = 128
BF16_ROWS = 2 * SUBLANES
MXU_DEPTH = 256
VMEM_LIMIT_BYTES = 52 * 1024 * 1024

CONV_TILE = 512
QKV_TILE = 512
ROUTE_TILE = 256
ROUTE_HEADS_PER_ITER = 4
DENSE_TILE = 512
EXPERT_CHUNK = 1024
DENSE_PIECES = 2
DOT_LAG = 1
SLAB_ROWS = 32
SLAB_I1 = 2


def _params(*semantics, flags=None):
    return pltpu.CompilerParams(dimension_semantics=semantics, vmem_limit_bytes=VMEM_LIMIT_BYTES,
                                flags=flags)


def _rmsnorm(x, g):
    ms = jnp.mean(x * x, axis=-1, keepdims=True)
    return x * lax.rsqrt(ms + RMS_EPS) * g


def _conv_in_kernel(x_ref, g_ref, w_ref, u_ref, gb_ref):
    d = x_ref.shape[1]
    xn = _rmsnorm(x_ref[...], g_ref[...]).astype(BF16)
    gb_ref[...] = jnp.dot(xn, w_ref[:, :d], preferred_element_type=F32)
    gate_c = jnp.dot(xn, w_ref[:, d:2 * d], preferred_element_type=F32)
    h = jnp.dot(xn, w_ref[:, 2 * d:], preferred_element_type=F32)
    u_ref[...] = gate_c * h


def _conv_in(x2, g, w_in):
    t, d = x2.shape
    tile = min(CONV_TILE, t)
    return pl.pallas_call(
        _conv_in_kernel,
        grid=(t // tile,),
        in_specs=[
            pl.BlockSpec((tile, d), lambda i: (i, 0)),
            pl.BlockSpec((1, d), lambda i: (0, 0)),
            pl.BlockSpec((d, 3 * d), lambda i: (0, 0)),
        ],
        out_specs=[
            pl.BlockSpec((tile, d), lambda i: (i, 0)),
            pl.BlockSpec((tile, d), lambda i: (i, 0)),
        ],
        out_shape=[jax.ShapeDtypeStruct((t, d), F32), jax.ShapeDtypeStruct((t, d), F32)],
        compiler_params=_params("parallel"),
        name="conv_in",
    )(x2, g, w_in)


def _conv_out_kernel(tiles_per_seq, u_ref, uprev_ref, unext_ref, gb_ref, x_ref, cw_ref, w_ref, o_ref):
    tile = u_ref.shape[0]
    pos = pl.program_id(0) % tiles_per_seq
    u = u_ref[...]
    prev_row = jnp.where(pos == 0, 0.0, uprev_ref[SUBLANES - 1:SUBLANES, :])
    next_row = jnp.where(pos == tiles_per_seq - 1, 0.0, unext_ref[0:1, :])
    row = lax.broadcasted_iota(jnp.int32, u.shape, 0)
    u_before = jnp.where(row == 0, prev_row, pltpu.roll(u, 1, axis=0))
    u_after = jnp.where(row == tile - 1, next_row, pltpu.roll(u, tile - 1, axis=0))
    y = cw_ref[0:1, :] * u_before + cw_ref[1:2, :] * u + cw_ref[2:3, :] * u_after
    z = (gb_ref[...] * y).astype(BF16)
    o_ref[...] = x_ref[...] + jnp.dot(z, w_ref[...], preferred_element_type=F32)


def _conv_out(u, gb, x2, conv_w, w_out, seq):
    t, d = x2.shape
    tile = min(CONV_TILE, seq)
    halo_per_tile = tile // SUBLANES
    n_halo = t // SUBLANES
    return pl.pallas_call(
        functools.partial(_conv_out_kernel, seq // tile),
        grid=(t // tile,),
        in_specs=[
            pl.BlockSpec((tile, d), lambda i: (i, 0)),
            pl.BlockSpec((SUBLANES, d), lambda i: (jnp.maximum(i * halo_per_tile - 1, 0), 0)),
            pl.BlockSpec((SUBLANES, d), lambda i: (jnp.minimum((i + 1) * halo_per_tile, n_halo - 1), 0)),
            pl.BlockSpec((tile, d), lambda i: (i, 0)),
            pl.BlockSpec((tile, d), lambda i: (i, 0)),
            pl.BlockSpec((CONV_WIDTH, d), lambda i: (0, 0)),
            pl.BlockSpec((d, d), lambda i: (0, 0)),
        ],
        out_specs=pl.BlockSpec((tile, d), lambda i: (i, 0)),
        out_shape=jax.ShapeDtypeStruct((t, d), F32),
        compiler_params=_params("parallel"),
        name="conv_out",
    )(u, u, u, gb, x2, conv_w, w_out)


def _attn_qkv_kernel(x_ref, g_ref, w_ref, q_ref, k_ref, v_ref):
    nq = q_ref.shape[1]
    nk = k_ref.shape[1]
    xn = _rmsnorm(x_ref[...], g_ref[...]).astype(BF16)
    q = jnp.dot(xn, w_ref[:, :nq], preferred_element_type=F32) * (1.0 / math.sqrt(HEAD_DIM))
    q_ref[...] = q.astype(BF16)
    k_ref[...] = jnp.dot(xn, w_ref[:, nq:nq + nk], preferred_element_type=F32).astype(BF16)
    v_ref[...] = jnp.dot(xn, w_ref[:, nq + nk:], preferred_element_type=F32).astype(BF16)


def _attn_qkv(x2, g, w_qkv):
    t, d = x2.shape
    tile = min(QKV_TILE, t)
    nq = N_HEADS * HEAD_DIM
    nk = N_KV_HEADS * HEAD_DIM
    return pl.pallas_call(
        _attn_qkv_kernel,
        grid=(t // tile,),
        in_specs=[
            pl.BlockSpec((tile, d), lambda i: (i, 0)),
            pl.BlockSpec((1, d), lambda i: (0, 0)),
            pl.BlockSpec((d, nq + 2 * nk), lambda i: (0, 0)),
        ],
        out_specs=[
            pl.BlockSpec((tile, nq), lambda i: (i, 0)),
            pl.BlockSpec((tile, nk), lambda i: (i, 0)),
            pl.BlockSpec((tile, nk), lambda i: (i, 0)),
        ],
        out_shape=[
            jax.ShapeDtypeStruct((t, nq), BF16),
            jax.ShapeDtypeStruct((t, nk), BF16),
            jax.ShapeDtypeStruct((t, nk), BF16),
        ],
        compiler_params=_params("parallel"),
        name="attn_qkv",
    )(x2, g, w_qkv)


def _attn_kernel(blocks_per_seq, q_ref, kp_ref, kc_ref, kn_ref, vp_ref, vc_ref, vn_ref, bucket_ref,
                 relb_ref, sink_ref, x_ref, wo_ref, o_ref, bias_ref, att_ref):
    step = pl.program_id(0)

    @pl.when(step == 0)
    def _():
        bucket = bucket_ref[...]
        rel = lax.broadcasted_iota(jnp.int32, bucket.shape, 1) - BLOCK - lax.broadcasted_iota(
            jnp.int32, bucket.shape, 0)
        in_window = jnp.abs(rel) <= WINDOW

        def head_body(hq, carry):
            def bucket_body(b, acc):
                return jnp.where(bucket == b, relb_ref[b, hq], acc)

            table = lax.fori_loop(0, N_BUCKETS, bucket_body, jnp.zeros(bucket.shape, F32))
            bias_ref[hq] = jnp.where(in_window, table, NEG_INF)
            return carry

        lax.fori_loop(0, N_HEADS, head_body, 0)

    pos = step % blocks_per_seq
    has_prev = pos > 0
    has_next = pos < blocks_per_seq - 1
    col = lax.broadcasted_iota(jnp.int32, (BLOCK, 3 * BLOCK), 1)
    key_ok = ((col >= BLOCK) | has_prev) & ((col < 2 * BLOCK) | has_next)

    for kvh in range(N_KV_HEADS):
        lo = kvh * HEAD_DIM
        k = jnp.concatenate([kp_ref[:, lo:lo + HEAD_DIM], kc_ref[:, lo:lo + HEAD_DIM],
                             kn_ref[:, lo:lo + HEAD_DIM]], axis=0)
        v = jnp.concatenate([vp_ref[:, lo:lo + HEAD_DIM], vc_ref[:, lo:lo + HEAD_DIM],
                             vn_ref[:, lo:lo + HEAD_DIM]], axis=0)
        for g in range(GROUP):
            hq = kvh * GROUP + g
            q = q_ref[:, hq * HEAD_DIM:(hq + 1) * HEAD_DIM]
            s = lax.dot_general(q, k, (((1,), (1,)), ((), ())), preferred_element_type=F32)
            logits = jnp.where(key_ok, s + bias_ref[hq], NEG_INF)
            sink = sink_ref[hq]
            m = jnp.maximum(jnp.max(logits, axis=-1, keepdims=True), sink)
            p = jnp.exp(logits - m)
            den = jnp.sum(p, axis=-1, keepdims=True) + jnp.exp(sink - m)
            o = jnp.dot(p.astype(BF16), v, preferred_element_type=F32) / den
            att_ref[:, hq * HEAD_DIM:(hq + 1) * HEAD_DIM] = o.astype(BF16)

    o_ref[...] = x_ref[...] + jnp.dot(att_ref[...], wo_ref[...], preferred_element_type=F32)


def _attn(q, k, v, bucket, rel_bias, sink, x2, w_o, seq):
    t, d = x2.shape
    nq = N_HEADS * HEAD_DIM
    nk = N_KV_HEADS * HEAD_DIM
    n_blocks = t // BLOCK
    prev_map = lambda i: (jnp.maximum(i - 1, 0), 0)
    cur_map = lambda i: (i, 0)
    next_map = lambda i: (jnp.minimum(i + 1, n_blocks - 1), 0)
    smem = pl.BlockSpec(memory_space=pltpu.SMEM)
    return pl.pallas_call(
        functools.partial(_attn_kernel, seq // BLOCK),
        grid=(n_blocks,),
        in_specs=[
            pl.BlockSpec((BLOCK, nq), cur_map),
            pl.BlockSpec((BLOCK, nk), prev_map),
            pl.BlockSpec((BLOCK, nk), cur_map),
            pl.BlockSpec((BLOCK, nk), next_map),
            pl.BlockSpec((BLOCK, nk), prev_map),
            pl.BlockSpec((BLOCK, nk), cur_map),
            pl.BlockSpec((BLOCK, nk), next_map),
            pl.BlockSpec((BLOCK, 3 * BLOCK), lambda i: (0, 0)),
            smem,
            smem,
            pl.BlockSpec((BLOCK, d), cur_map),
            pl.BlockSpec((nq, d), lambda i: (0, 0)),
        ],
        out_specs=pl.BlockSpec((BLOCK, d), cur_map),
        out_shape=jax.ShapeDtypeStruct((t, d), F32),
        scratch_shapes=[
            pltpu.VMEM((N_HEADS, BLOCK, 3 * BLOCK), F32),
            pltpu.VMEM((BLOCK, nq), BF16),
        ],
        compiler_params=_params("arbitrary"),
        name="attn",
    )(q, k, k, k, v, v, v, bucket, rel_bias, sink, x2, w_o)


def _t5_bucket_table():
    half = N_BUCKETS // 2
    max_exact = half // 2
    qi = jnp.arange(BLOCK)[:, None]
    kj = jnp.arange(3 * BLOCK)[None, :]
    rel = kj - BLOCK - qi
    ret = jnp.where(rel > 0, half, 0)
    n = jnp.abs(rel)
    nf = jnp.maximum(n, 1).astype(jnp.float32)
    large = max_exact + (jnp.log(nf / max_exact) / math.log(MAX_DISTANCE / max_exact)
                         * (half - max_exact)).astype(jnp.int32)
    large = jnp.minimum(large, half - 1)
    return (ret + jnp.where(n < max_exact, n, large)).astype(jnp.int32)


INT32_MIN = -(2 ** 31)


def _ordered_int(bits):
    return bits ^ ((bits >> 31) & 0x7FFFFFFF)


def _top16_ranks(s, key_iota):
    bits = lax.bitcast_convert_type(s, jnp.int32)
    keys = _ordered_int(jnp.where(bits == INT32_MIN, 0, bits))
    slot = lax.broadcasted_iota(jnp.int32, (PEER_TOPK, s.shape[1]), 0)
    top = jnp.zeros((PEER_TOPK, s.shape[1]), jnp.int32)
    for r in range(PEER_TOPK):
        m = jnp.max(keys, axis=0, keepdims=True)
        first = jnp.min(jnp.where(keys == m, key_iota, N_KEYS), axis=0, keepdims=True)
        keys = jnp.where(key_iota == first, INT32_MIN + r, keys)
        top = jnp.where(slot == r, m, top)
    rank = jnp.where(keys < INT32_MIN + PEER_TOPK, keys ^ INT32_MIN, PEER_TOPK).astype(F32)
    return rank, lax.bitcast_convert_type(_ordered_int(top), F32)


def _staircase(sv1, sv2):
    lanes = sv1.shape[1]
    slot = lax.broadcasted_iota(jnp.int32, (PEER_TOPK, lanes), 0)
    n = jnp.zeros((PEER_TOPK, lanes), jnp.int32)
    front = sv1[0:1, :] + sv2
    big = PEER_TOPK * PEER_TOPK + PEER_TOPK
    for r in range(PEER_TOPK):
        m = jnp.max(front, axis=0, keepdims=True)
        flat = n * PEER_TOPK + slot
        first = jnp.min(jnp.where(front == m, flat, big), axis=0, keepdims=True)
        chosen = flat == first
        n = n + chosen.astype(jnp.int32)
        if r + 1 < PEER_TOPK:
            depth = jnp.sum(jnp.where(chosen, n, 0), axis=0, keepdims=True)
            nxt = jnp.sum(jnp.where(slot == depth, sv1, 0.0), axis=0, keepdims=True)
            front = jnp.where(chosen, jnp.where(depth < PEER_TOPK, nxt + sv2, -jnp.inf), front)
    return n.astype(F32)


def _twice_bf16(v):
    bits = lax.bitcast_convert_type(v.astype(GATE_DTYPE).astype(F32), jnp.uint32)
    return lax.bitcast_convert_type(bits | (bits >> 16), jnp.int32)


def _peer_route_kernel(x_ref, g_ref, wq_ref, keys_ref, xn_ref, r1_ref, c_ref, n2_ref, w_ref, s_ref):
    tile = x_ref.shape[0]
    xn_t = _rmsnorm(x_ref[...], g_ref[...]).T.astype(BF16)
    xn_ref[...] = xn_t
    q_t = jnp.dot(wq_ref[...], xn_t, preferred_element_type=F32)
    for hc in range(PEER_HEADS * 2):
        q_hc = q_t[hc * SUB_DIM:(hc + 1) * SUB_DIM, :].astype(BF16)
        s_ref[hc] = jnp.dot(keys_ref[hc], q_hc, preferred_element_type=F32)

    key_iota = lax.broadcasted_iota(jnp.int32, (N_KEYS, LANES), 0)
    slot = lax.broadcasted_iota(jnp.int32, (PEER_TOPK, LANES), 0).astype(F32)

    def route_head(hd, lane_chunk):
        lane0 = pl.multiple_of(lane_chunk * LANES, LANES)
        lanes = pl.ds(lane0, LANES)
        s1 = s_ref[2 * hd, :, lanes]
        s2 = s_ref[2 * hd + 1, :, lanes]
        rank1, sv1 = _top16_ranks(s1, key_iota)
        rank2, sv2 = _top16_ranks(s2, key_iota)
        n = _staircase(sv1, sv2)
        e1 = jnp.exp(s1 - sv1[0:1, :])
        e2 = jnp.exp(s2 - sv2[0:1, :])
        e1s = jnp.exp(sv1 - sv1[0:1, :])
        e2s = jnp.exp(sv2 - sv2[0:1, :])
        inner = jnp.zeros_like(e2s)
        for p in range(PEER_TOPK):
            inner = inner + jnp.where(n > float(p), e1s[p:p + 1, :], 0.0)
        z = jnp.sum(inner * e2s, axis=0, keepdims=True)
        n2 = jnp.zeros_like(rank2)
        for q in range(PEER_TOPK):
            n2 = jnp.where(rank2 == float(q), n[q:q + 1, :], n2)
        r1_ref[hd, lane_chunk] = _twice_bf16(rank1)
        c_ref[hd, lane_chunk] = _twice_bf16(e1 / z)
        n2_ref[hd, :, lanes] = n2.astype(GATE_DTYPE)
        w_ref[hd, :, lanes] = e2.astype(GATE_DTYPE)

    def body(it, carry):
        lane_chunk = it % (tile // LANES)
        for i in range(ROUTE_HEADS_PER_ITER):
            route_head((it // (tile // LANES)) * ROUTE_HEADS_PER_ITER + i, lane_chunk)
        return carry

    lax.fori_loop(0, (PEER_HEADS // ROUTE_HEADS_PER_ITER) * (tile // LANES), body, 0)


def _peer_route(x2, g, wq_t, keys):
    t, d = x2.shape
    tile = min(ROUTE_TILE, t)
    nqd = PEER_HEADS * 2 * SUB_DIM
    scalar_shape = jax.ShapeDtypeStruct((PEER_HEADS, t // LANES, N_KEYS, LANES), jnp.int32)
    scalar_spec = pl.BlockSpec((PEER_HEADS, tile // LANES, N_KEYS, LANES), lambda i: (0, i, 0, 0))
    table_shape = jax.ShapeDtypeStruct((PEER_HEADS, N_KEYS, t), GATE_DTYPE)
    table_spec = pl.BlockSpec((PEER_HEADS, N_KEYS, tile), lambda i: (0, 0, i))
    return pl.pallas_call(
        _peer_route_kernel,
        grid=(t // tile,),
        in_specs=[
            pl.BlockSpec((tile, d), lambda i: (i, 0)),
            pl.BlockSpec((1, d), lambda i: (0, 0)),
            pl.BlockSpec((nqd, d), lambda i: (0, 0)),
            pl.BlockSpec((PEER_HEADS * 2, N_KEYS, SUB_DIM), lambda i: (0, 0, 0)),
        ],
        out_specs=[pl.BlockSpec((d, tile), lambda i: (0, i)), scalar_spec, scalar_spec, table_spec, table_spec],
        out_shape=[jax.ShapeDtypeStruct((d, t), BF16), scalar_shape, scalar_shape, table_shape, table_shape],
        scratch_shapes=[pltpu.VMEM((PEER_HEADS * 2, N_KEYS, tile), F32)],
        compiler_params=_params("parallel"),
        name="peer_route",
    )(x2, g, wq_t, keys)


def _plane_dot(lhs_planes, rows, rhs_ref, p, after=None):
    lhs = lhs_planes[p][rows, :]
    if after is not None:
        words = pltpu.bitcast(lhs, jnp.int32)
        reps = (words.shape[0] // after.shape[0], words.shape[1] // after.shape[1])
        lhs = pltpu.bitcast(words | jnp.tile(after, reps), lhs.dtype)
    return jnp.dot(lhs, rhs_ref[p * MXU_DEPTH:(p + 1) * MXU_DEPTH, :], preferred_element_type=F32)


def _k_planes_of_transpose(table):
    k, m = table.shape
    return jnp.swapaxes(table.astype(BF16).reshape(k // MXU_DEPTH, MXU_DEPTH, m), 1, 2)


def _zero_after(v):
    bits = lax.bitcast_convert_type(v, jnp.uint32)
    return lax.bitcast_convert_type((bits >> 16) >> 16, jnp.int32)


def _replicated_row(ref, lead, row):
    return ref[(*lead, pl.ds(row, SUBLANES, stride=0), slice(None))]


def _peer_dense_kernel(final_norm, n_chunks, n_planes, xn_ref, x_ref, *refs):
    u_planes = refs[:n_planes]
    vt_ref, r1_ref, c_ref, n2_ref, w_ref, fg_ref, o_ref, acc_ref, h0_ref, h1_ref, a0_ref, a1_ref = refs[n_planes:]
    vt_planes = [vt_ref.at[p] for p in range(n_planes)]
    s = pl.program_id(0)
    tile = xn_ref.shape[1]
    n_pieces = DENSE_PIECES
    up_rows = u_planes[0].shape[0] // n_pieces
    down_rows = vt_ref.shape[1] // n_pieces
    key_rows = N_KEYS // n_pieces
    rows_per_chunk = u_planes[0].shape[0] // N_KEYS
    slot = s % 2
    down_chunk = (s - 2) % n_chunks

    @pl.when(s == 0)
    def _():
        h1_ref[...] = jnp.zeros_like(h1_ref)
        a0_ref[...] = jnp.zeros_like(a0_ref)

    @pl.when((s < 2) | (down_chunk == 0))
    def _():
        acc_ref[...] = jnp.zeros_like(acc_ref)

    def run(h_up, h_gate, a_gate, a_down):
        def piece_body(k, carry):
            up = pl.ds(pl.multiple_of(k * up_rows, up_rows), up_rows)
            down = pl.ds(pl.multiple_of(k * down_rows, down_rows), down_rows)
            slabs = [(jg, sub) for jg in range(rows_per_chunk // SLAB_I1) for sub in range(key_rows // SLAB_ROWS)]
            slabs_per_dot = len(slabs) // (2 * n_planes)
            slab = (SLAB_ROWS // BF16_ROWS, BF16_ROWS, tile)

            def scalar_rows(ref, hd, j, anchor):
                words = [_replicated_row(ref, (hd, lc), j) for lc in range(tile // LANES)]
                if anchor is not None:
                    words[-1] = words[-1] | anchor
                return jnp.concatenate([pltpu.bitcast(wd, GATE_DTYPE) for wd in words], axis=1)[None]

            def gate_slab(jg, sub, anchor):
                key0 = pl.multiple_of(k * key_rows + sub * SLAB_ROWS, SLAB_ROWS)
                keys = pl.ds(key0, SLAB_ROWS)
                i1_rows = range(jg * SLAB_I1, (jg + 1) * SLAB_I1)
                gates = [jnp.zeros(slab, GATE_DTYPE) for _ in i1_rows]
                for hd in range(PEER_HEADS):
                    n2 = n2_ref[hd, keys, :].reshape(slab)
                    w = w_ref[hd, keys, :].reshape(slab)
                    for idx, j in enumerate(i1_rows):
                        hold = anchor if (hd == PEER_HEADS - 1 and idx == SLAB_I1 - 1) else None
                        r1 = scalar_rows(r1_ref, hd, j, hold)
                        c = scalar_rows(c_ref, hd, j, None)
                        gates[idx] = gates[idx] + jnp.where(r1 < n2, w, jnp.zeros_like(w)) * c
                for idx, j in enumerate(i1_rows):
                    rows = pl.ds(pl.multiple_of(j * N_KEYS + key0, SLAB_ROWS), SLAB_ROWS)
                    h = h_gate[rows, :]
                    act = (0.5 * h * (1.0 + lax.erf(h * (1.0 / math.sqrt(2.0))))).astype(GATE_DTYPE)
                    out = (gates[idx].reshape(SLAB_ROWS, tile) * act).astype(a_gate.dtype)
                    a_gate[rows, :] = out
                return _zero_after(pltpu.bitcast(out[:, :LANES], jnp.int32))

            dots = ([("down", vt_planes, down, a_down, p) for p in range(n_planes)]
                    + [("up", u_planes, up, xn_ref, p) for p in range(n_planes)])
            sums = {}
            tokens = []
            for i, (name, lhs_planes, rows, rhs_ref, p) in enumerate(dots):
                after = tokens[i - DOT_LAG] if i >= DOT_LAG else None
                part = _plane_dot(lhs_planes, rows, rhs_ref, p, after)
                sums[name] = part if p == 0 else sums[name] + part
                if (name, p) == ("up", n_planes - 1):
                    h_up[up, :] = sums["up"]
                if (name, p) == ("down", n_planes - 1):
                    acc_ref[down, :] += sums["down"]
                for j, sub in slabs[i * slabs_per_dot:(i + 1) * slabs_per_dot]:
                    anchor = None
                    if (j, sub) == slabs[-1]:
                        anchor = (_zero_after(sums["up"][-SUBLANES:, :LANES])
                                  | _zero_after(sums["up"][-SUBLANES:, -LANES:])
                                  | _zero_after(sums["down"][-SUBLANES:, :LANES])
                                  | _zero_after(sums["down"][-SUBLANES:, -LANES:]))
                    token = gate_slab(j, sub, anchor)
                tokens.append(token)
            return carry

        lax.fori_loop(0, n_pieces, piece_body, 0)

    @pl.when(slot == 0)
    def _():
        run(h0_ref, h1_ref, a1_ref, a0_ref)

    @pl.when(slot == 1)
    def _():
        run(h1_ref, h0_ref, a0_ref, a1_ref)

    @pl.when((s >= 2) & (down_chunk == n_chunks - 1))
    def _():
        y = x_ref[...] + acc_ref[...].T
        if final_norm:
            y = _rmsnorm(y, fg_ref[...])
        o_ref[...] = y


def _peer_dense(xn, x2, u, vt, r1, c, n2, w, final_g, final_norm):
    t, d = x2.shape
    n_experts = u.shape[0]
    n_planes = d // MXU_DEPTH
    tile = min(DENSE_TILE, t)
    ec = EXPERT_CHUNK
    rows = ec // N_KEYS
    n_chunks = n_experts // ec
    n_pos = (t // tile) * n_chunks

    def pos(s, lag):
        p = jnp.clip(s - lag, 0, n_pos - 1)
        return p // n_chunks, p % n_chunks

    up_tile = lambda s: pos(s, 0)[0]
    up_chunk = lambda s: pos(s, 0)[1]
    gate_tile = lambda s: pos(s, 1)[0]
    gate_chunk = lambda s: pos(s, 1)[1]
    down_tile = lambda s: pos(s, 2)[0]
    down_chunk = lambda s: pos(s, 2)[1]
    return pl.pallas_call(
        functools.partial(_peer_dense_kernel, final_norm, n_chunks, n_planes),
        grid=(n_pos + 2,),
        in_specs=[
            pl.BlockSpec((d, tile), lambda s: (0, up_tile(s))),
            pl.BlockSpec((tile, d), lambda s: (down_tile(s), 0)),
            *[pl.BlockSpec((ec, MXU_DEPTH), functools.partial(lambda p, s: (up_chunk(s), p), p))
              for p in range(n_planes)],
            pl.BlockSpec((ec // MXU_DEPTH, d, MXU_DEPTH), lambda s: (down_chunk(s), 0, 0)),
            pl.BlockSpec((PEER_HEADS, tile // LANES, rows, LANES), lambda s: (0, gate_tile(s), gate_chunk(s), 0)),
            pl.BlockSpec((PEER_HEADS, tile // LANES, rows, LANES), lambda s: (0, gate_tile(s), gate_chunk(s), 0)),
            pl.BlockSpec((PEER_HEADS, N_KEYS, tile), lambda s: (0, 0, gate_tile(s))),
            pl.BlockSpec((PEER_HEADS, N_KEYS, tile), lambda s: (0, 0, gate_tile(s))),
            pl.BlockSpec((1, d), lambda s: (0, 0)),
        ],
        out_specs=pl.BlockSpec((tile, d), lambda s: (down_tile(s), 0)),
        out_shape=jax.ShapeDtypeStruct((t, d), F32),
        scratch_shapes=[
            pltpu.VMEM((d, tile), F32),
            pltpu.VMEM((ec, tile), F32),
            pltpu.VMEM((ec, tile), F32),
            pltpu.VMEM((ec, tile), BF16),
            pltpu.VMEM((ec, tile), BF16),
        ],
        compiler_params=_params("arbitrary"),
        name="peer_dense",
    )(xn, x2, *([u] * n_planes), vt, r1, c, n2, w, final_g)


def _peer(x2, g, w_q, subkeys, u_tab, v_tab, final_g, final_norm):
    d = x2.shape[1]
    wq_t = w_q.astype(BF16).T
    keys = subkeys.reshape(PEER_HEADS * 2, N_KEYS, SUB_DIM).astype(BF16)
    xn, r1, c, n2, w = _peer_route(x2, g.reshape(1, d), wq_t, keys)
    return _peer_dense(xn, x2, u_tab.astype(BF16), _k_planes_of_transpose(v_tab), r1, c, n2, w,
                       final_g.reshape(1, d), final_norm)


def kernel(x, conv_norm_g, conv_w_in, conv_w, conv_w_out, attn_norm_g, attn_w_qkv, attn_sink, attn_w_o,
           rel_bias, ffn_norm_g, peer_w_q, peer_subkeys, peer_u, peer_v, final_norm_g):
    bsz, seq, d = x.shape
    depth = ffn_norm_g.shape[0]
    x2 = x.reshape(bsz * seq, d)
    bucket = _t5_bucket_table()
    for i in range(depth):
        j = i // 2
        if i % 2 == 0:
            u, gb = _conv_in(x2, conv_norm_g[j].reshape(1, d), conv_w_in[j].astype(BF16))
            x2 = _conv_out(u, gb, x2, conv_w[j], conv_w_out[j].astype(BF16), seq)
        else:
            q, k, v = _attn_qkv(x2, attn_norm_g[j].reshape(1, d), attn_w_qkv[j].astype(BF16))
            x2 = _attn(q, k, v, bucket, rel_bias, attn_sink[j], x2, attn_w_o[j].astype(BF16), seq)
        x2 = _peer(x2, ffn_norm_g[i], peer_w_q[i], peer_subkeys[i], peer_u[i], peer_v[i],
                   final_norm_g, i == depth - 1)
    return x2.reshape(bsz, seq, d)
```

```python
import functools
import math

import jax
import jax.numpy as jnp
import numpy as np
from jax import lax
from jax.experimental import pallas as pl
from jax.experimental.pallas import tpu as pltpu

F32 = jnp.float32
BF16 = jnp.bfloat16
GATE_DTYPE = jnp.bfloat16

RMS_EPS = 1e-6
CONV_WIDTH = 3
N_HEADS = 16
N_KV_HEADS = 4
HEAD_DIM = 64
GROUP = N_HEADS // N_KV_HEADS
WINDOW = 128
BLOCK = 128
NEG_INF = -1e30
N_BUCKETS = 32
MAX_DISTANCE = 128
PEER_HEADS = 8
N_KEYS = 128
PEER_TOPK = 16
SUB_DIM = 128

SUBLANES = 8
LANES = 128
BF16_ROWS = 2 * SUBLANES
MXU_DEPTH = 256
VMEM_LIMIT_BYTES = 52 * 1024 * 1024

CONV_TILE = 512
QKV_TILE = 512
ROUTE_TILE = 256
ROUTE_HEADS_PER_ITER = 8
DENSE_TILE = 512
EXPERT_CHUNK = 1024
DENSE_PIECES = 2
DOT_LAG = 1
SLAB_ROWS = 32
SLAB_I1 = 2


def _params(*semantics, flags=None):
    return pltpu.CompilerParams(dimension_semantics=semantics, vmem_limit_bytes=VMEM_LIMIT_BYTES,
                                flags=flags)


def _rmsnorm(x, g):
    ms = jnp.mean(x * x, axis=-1, keepdims=True)
    return x * lax.rsqrt(ms + RMS_EPS) * g


def _conv_in_kernel(x_ref, g_ref, w_ref, u_ref, gb_ref):
    d = x_ref.shape[1]
    xn = _rmsnorm(x_ref[...], g_ref[...]).astype(BF16)
    gb_ref[...] = jnp.dot(xn, w_ref[:, :d], preferred_element_type=F32)
    gate_c = jnp.dot(xn, w_ref[:, d:2 * d], preferred_element_type=F32)
    h = jnp.dot(xn, w_ref[:, 2 * d:], preferred_element_type=F32)
    u_ref[...] = gate_c * h


def _conv_in(x2, g, w_in):
    t, d = x2.shape
    tile = min(CONV_TILE, t)
    return pl.pallas_call(
        _conv_in_kernel,
        grid=(t // tile,),
        in_specs=[
            pl.BlockSpec((tile, d), lambda i: (i, 0)),
            pl.BlockSpec((1, d), lambda i: (0, 0)),
            pl.BlockSpec((d, 3 * d), lambda i: (0, 0)),
        ],
        out_specs=[
            pl.BlockSpec((tile, d), lambda i: (i, 0)),
            pl.BlockSpec((tile, d), lambda i: (i, 0)),
        ],
        out_shape=[jax.ShapeDtypeStruct((t, d), F32), jax.ShapeDtypeStruct((t, d), F32)],
        compiler_params=_params("parallel"),
        name="conv_in",
    )(x2, g, w_in)


def _conv_out_kernel(tiles_per_seq, u_ref, uprev_ref, unext_ref, gb_ref, x_ref, cw_ref, w_ref, o_ref):
    tile = u_ref.shape[0]
    pos = pl.program_id(0) % tiles_per_seq
    u = u_ref[...]
    prev_row = jnp.where(pos == 0, 0.0, uprev_ref[SUBLANES - 1:SUBLANES, :])
    next_row = jnp.where(pos == tiles_per_seq - 1, 0.0, unext_ref[0:1, :])
    row = lax.broadcasted_iota(jnp.int32, u.shape, 0)
    u_before = jnp.where(row == 0, prev_row, pltpu.roll(u, 1, axis=0))
    u_after = jnp.where(row == tile - 1, next_row, pltpu.roll(u, tile - 1, axis=0))
    y = cw_ref[0:1, :] * u_before + cw_ref[1:2, :] * u + cw_ref[2:3, :] * u_after
    z = (gb_ref[...] * y).astype(BF16)
    o_ref[...] = x_ref[...] + jnp.dot(z, w_ref[...], preferred_element_type=F32)


def _conv_out(u, gb, x2, conv_w, w_out, seq):
    t, d = x2.shape
    tile = min(CONV_TILE, seq)
    halo_per_tile = tile // SUBLANES
    n_halo = t // SUBLANES
    return pl.pallas_call(
        functools.partial(_conv_out_kernel, seq // tile),
        grid=(t // tile,),
        in_specs=[
            pl.BlockSpec((tile, d), lambda i: (i, 0)),
            pl.BlockSpec((SUBLANES, d), lambda i: (jnp.maximum(i * halo_per_tile - 1, 0), 0)),
            pl.BlockSpec((SUBLANES, d), lambda i: (jnp.minimum((i + 1) * halo_per_tile, n_halo - 1), 0)),
            pl.BlockSpec((tile, d), lambda i: (i, 0)),
            pl.BlockSpec((tile, d), lambda i: (i, 0)),
            pl.BlockSpec((CONV_WIDTH, d), lambda i: (0, 0)),
            pl.BlockSpec((d, d), lambda i: (0, 0)),
        ],
        out_specs=pl.BlockSpec((tile, d), lambda i: (i, 0)),
        out_shape=jax.ShapeDtypeStruct((t, d), F32),
        compiler_params=_params("parallel"),
        name="conv_out",
    )(u, u, u, gb, x2, conv_w, w_out)


def _attn_qkv_kernel(x_ref, g_ref, w_ref, q_ref, k_ref, v_ref):
    nq = q_ref.shape[1]
    nk = k_ref.shape[1]
    xn = _rmsnorm(x_ref[...], g_ref[...]).astype(BF16)
    q = jnp.dot(xn, w_ref[:, :nq], preferred_element_type=F32) * (1.0 / math.sqrt(HEAD_DIM))
    q_ref[...] = q.astype(BF16)
    k_ref[...] = jnp.dot(xn, w_ref[:, nq:nq + nk], preferred_element_type=F32).astype(BF16)
    v_ref[...] = jnp.dot(xn, w_ref[:, nq + nk:], preferred_element_type=F32).astype(BF16)


def _attn_qkv(x2, g, w_qkv):
    t, d = x2.shape
    tile = min(QKV_TILE, t)
    nq = N_HEADS * HEAD_DIM
    nk = N_KV_HEADS * HEAD_DIM
    return pl.pallas_call(
        _attn_qkv_kernel,
        grid=(t // tile,),
        in_specs=[
            pl.BlockSpec((tile, d), lambda i: (i, 0)),
            pl.BlockSpec((1, d), lambda i: (0, 0)),
            pl.BlockSpec((d, nq + 2 * nk), lambda i: (0, 0)),
        ],
        out_specs=[
            pl.BlockSpec((tile, nq), lambda i: (i, 0)),
            pl.BlockSpec((tile, nk), lambda i: (i, 0)),
            pl.BlockSpec((tile, nk), lambda i: (i, 0)),
        ],
        out_shape=[
            jax.ShapeDtypeStruct((t, nq), BF16),
            jax.ShapeDtypeStruct((t, nk), BF16),
            jax.ShapeDtypeStruct((t, nk), BF16),
        ],
        compiler_params=_params("parallel"),
        name="attn_qkv",
    )(x2, g, w_qkv)


def _attn_kernel(blocks_per_seq, q_ref, kp_ref, kc_ref, kn_ref, vp_ref, vc_ref, vn_ref, bucket_ref,
                 relb_ref, sink_ref, x_ref, wo_ref, o_ref, bias_ref, att_ref):
    step = pl.program_id(0)

    @pl.when(step == 0)
    def _():
        bucket = bucket_ref[...]
        rel = lax.broadcasted_iota(jnp.int32, bucket.shape, 1) - BLOCK - lax.broadcasted_iota(
            jnp.int32, bucket.shape, 0)
        in_window = jnp.abs(rel) <= WINDOW

        def head_body(hq, carry):
            def bucket_body(b, acc):
                return jnp.where(bucket == b, relb_ref[b, hq], acc)

            table = lax.fori_loop(0, N_BUCKETS, bucket_body, jnp.zeros(bucket.shape, F32))
            bias_ref[hq] = jnp.where(in_window, table, NEG_INF)
            return carry

        lax.fori_loop(0, N_HEADS, head_body, 0)

    pos = step % blocks_per_seq
    has_prev = pos > 0
    has_next = pos < blocks_per_seq - 1
    col = lax.broadcasted_iota(jnp.int32, (BLOCK, 3 * BLOCK), 1)
    key_ok = ((col >= BLOCK) | has_prev) & ((col < 2 * BLOCK) | has_next)

    for kvh in range(N_KV_HEADS):
        lo = kvh * HEAD_DIM
        k = jnp.concatenate([kp_ref[:, lo:lo + HEAD_DIM], kc_ref[:, lo:lo + HEAD_DIM],
                             kn_ref[:, lo:lo + HEAD_DIM]], axis=0)
        v = jnp.concatenate([vp_ref[:, lo:lo + HEAD_DIM], vc_ref[:, lo:lo + HEAD_DIM],
                             vn_ref[:, lo:lo + HEAD_DIM]], axis=0)
        for g in range(GROUP):
            hq = kvh * GROUP + g
            q = q_ref[:, hq * HEAD_DIM:(hq + 1) * HEAD_DIM]
            s = lax.dot_general(q, k, (((1,), (1,)), ((), ())), preferred_element_type=F32)
            logits = jnp.where(key_ok, s + bias_ref[hq], NEG_INF)
            sink = sink_ref[hq]
            m = jnp.maximum(jnp.max(logits, axis=-1, keepdims=True), sink)
            p = jnp.exp(logits - m)
            den = jnp.sum(p, axis=-1, keepdims=True) + jnp.exp(sink - m)
            o = jnp.dot(p.astype(BF16), v, preferred_element_type=F32) / den
            att_ref[:, hq * HEAD_DIM:(hq + 1) * HEAD_DIM] = o.astype(BF16)

    o_ref[...] = x_ref[...] + jnp.dot(att_ref[...], wo_ref[...], preferred_element_type=F32)


def _attn(q, k, v, bucket, rel_bias, sink, x2, w_o, seq):
    t, d = x2.shape
    nq = N_HEADS * HEAD_DIM
    nk = N_KV_HEADS * HEAD_DIM
    n_blocks = t // BLOCK
    prev_map = lambda i: (jnp.maximum(i - 1, 0), 0)
    cur_map = lambda i: (i, 0)
    next_map = lambda i: (jnp.minimum(i + 1, n_blocks - 1), 0)
    smem = pl.BlockSpec(memory_space=pltpu.SMEM)
    return pl.pallas_call(
        functools.partial(_attn_kernel, seq // BLOCK),
        grid=(n_blocks,),
        in_specs=[
            pl.BlockSpec((BLOCK, nq), cur_map),
            pl.BlockSpec((BLOCK, nk), prev_map),
            pl.BlockSpec((BLOCK, nk), cur_map),
            pl.BlockSpec((BLOCK, nk), next_map),
            pl.BlockSpec((BLOCK, nk), prev_map),
            pl.BlockSpec((BLOCK, nk), cur_map),
            pl.BlockSpec((BLOCK, nk), next_map),
            pl.BlockSpec((BLOCK, 3 * BLOCK), lambda i: (0, 0)),
            smem,
            smem,
            pl.BlockSpec((BLOCK, d), cur_map),
            pl.BlockSpec((nq, d), lambda i: (0, 0)),
        ],
        out_specs=pl.BlockSpec((BLOCK, d), cur_map),
        out_shape=jax.ShapeDtypeStruct((t, d), F32),
        scratch_shapes=[
            pltpu.VMEM((N_HEADS, BLOCK, 3 * BLOCK), F32),
            pltpu.VMEM((BLOCK, nq), BF16),
        ],
        compiler_params=_params("arbitrary"),
        name="attn",
    )(q, k, k, k, v, v, v, bucket, rel_bias, sink, x2, w_o)


def _t5_bucket_table():
    half = N_BUCKETS // 2
    max_exact = half // 2
    qi = jnp.arange(BLOCK)[:, None]
    kj = jnp.arange(3 * BLOCK)[None, :]
    rel = kj - BLOCK - qi
    ret = jnp.where(rel > 0, half, 0)
    n = jnp.abs(rel)
    nf = jnp.maximum(n, 1).astype(jnp.float32)
    large = max_exact + (jnp.log(nf / max_exact) / math.log(MAX_DISTANCE / max_exact)
                         * (half - max_exact)).astype(jnp.int32)
    large = jnp.minimum(large, half - 1)
    return (ret + jnp.where(n < max_exact, n, large)).astype(jnp.int32)


INT32_MIN = -(2 ** 31)


def _ordered_int(bits):
    return bits ^ ((bits >> 31) & 0x7FFFFFFF)


def _top16_ranks(s, key_iota):
    bits = lax.bitcast_convert_type(s, jnp.int32)
    keys = _ordered_int(jnp.where(bits == INT32_MIN, 0, bits))
    slot = lax.broadcasted_iota(jnp.int32, (PEER_TOPK, s.shape[1]), 0)
    top = jnp.zeros((PEER_TOPK, s.shape[1]), jnp.int32)
    for r in range(PEER_TOPK):
        m = jnp.max(keys, axis=0, keepdims=True)
        first = jnp.min(jnp.where(keys == m, key_iota, N_KEYS), axis=0, keepdims=True)
        keys = jnp.where(key_iota == first, INT32_MIN + r, keys)
        top = jnp.where(slot == r, m, top)
    rank = jnp.where(keys < INT32_MIN + PEER_TOPK, keys ^ INT32_MIN, PEER_TOPK).astype(F32)
    return rank, lax.bitcast_convert_type(_ordered_int(top), F32)


def _staircase(sv1, sv2):
    lanes = sv1.shape[1]
    slot = lax.broadcasted_iota(jnp.int32, (PEER_TOPK, lanes), 0)
    n = jnp.zeros((PEER_TOPK, lanes), jnp.int32)
    front = sv1[0:1, :] + sv2
    big = PEER_TOPK * PEER_TOPK + PEER_TOPK
    for r in range(PEER_TOPK):
        m = jnp.max(front, axis=0, keepdims=True)
        flat = n * PEER_TOPK + slot
        first = jnp.min(jnp.where(front == m, flat, big), axis=0, keepdims=True)
        chosen = flat == first
        n = n + chosen.astype(jnp.int32)
        if r + 1 < PEER_TOPK:
            depth = jnp.sum(jnp.where(chosen, n, 0), axis=0, keepdims=True)
            nxt = jnp.sum(jnp.where(slot == depth, sv1, 0.0), axis=0, keepdims=True)
            front = jnp.where(chosen, jnp.where(depth < PEER_TOPK, nxt + sv2, -jnp.inf), front)
    return n.astype(F32)


def _twice_bf16(v):
    bits = lax.bitcast_convert_type(v.astype(GATE_DTYPE).astype(F32), jnp.uint32)
    return lax.bitcast_convert_type(bits | (bits >> 16), jnp.int32)


def _peer_route_kernel(x_ref, g_ref, wq_ref, keys_ref, xn_ref, r1_ref, c_ref, n2_ref, w_ref, s_ref):
    tile = x_ref.shape[0]
    xn_t = _rmsnorm(x_ref[...], g_ref[...]).T.astype(BF16)
    xn_ref[...] = xn_t
    q_t = jnp.dot(wq_ref[...], xn_t, preferred_element_type=F32)
    for hc in range(PEER_HEADS * 2):
        q_hc = q_t[hc * SUB_DIM:(hc + 1) * SUB_DIM, :].astype(BF16)
        s_ref[hc] = jnp.dot(keys_ref[hc], q_hc, preferred_element_type=F32)

    key_iota = lax.broadcasted_iota(jnp.int32, (N_KEYS, LANES), 0)
    slot = lax.broadcasted_iota(jnp.int32, (PEER_TOPK, LANES), 0).astype(F32)

    def route_head(hd, lane_chunk):
        lane0 = pl.multiple_of(lane_chunk * LANES, LANES)
        lanes = pl.ds(lane0, LANES)
        s1 = s_ref[2 * hd, :, lanes]
        s2 = s_ref[2 * hd + 1, :, lanes]
        rank1, sv1 = _top16_ranks(s1, key_iota)
        rank2, sv2 = _top16_ranks(s2, key_iota)
        n = _staircase(sv1, sv2)
        e1 = jnp.exp(s1 - sv1[0:1, :])
        e2 = jnp.exp(s2 - sv2[0:1, :])
        e1s = jnp.exp(sv1 - sv1[0:1, :])
        e2s = jnp.exp(sv2 - sv2[0:1, :])
        inner = jnp.zeros_like(e2s)
        for p in range(PEER_TOPK):
            inner = inner + jnp.where(n > float(p), e1s[p:p + 1, :], 0.0)
        z = jnp.sum(inner * e2s, axis=0, keepdims=True)
        n2 = jnp.zeros_like(rank2)
        for q in range(PEER_TOPK):
            n2 = jnp.where(rank2 == float(q), n[q:q + 1, :], n2)
        r1_ref[hd, lane_chunk] = _twice_bf16(rank1)
        c_ref[hd, lane_chunk] = _twice_bf16(e1 / z)
        n2_ref[hd, :, lanes] = n2.astype(GATE_DTYPE)
        w_ref[hd, :, lanes] = e2.astype(GATE_DTYPE)

    def body(it, carry):
        lane_chunk = it % (tile // LANES)
        for i in range(ROUTE_HEADS_PER_ITER):
            route_head((it // (tile // LANES)) * ROUTE_HEADS_PER_ITER + i, lane_chunk)
        return carry

    lax.fori_loop(0, (PEER_HEADS // ROUTE_HEADS_PER_ITER) * (tile // LANES), body, 0)


def _peer_route(x2, g, wq_t, keys):
    t, d = x2.shape
    tile = min(ROUTE_TILE, t)
    nqd = PEER_HEADS * 2 * SUB_DIM
    scalar_shape = jax.ShapeDtypeStruct((PEER_HEADS, t // LANES, N_KEYS, LANES), jnp.int32)
    scalar_spec = pl.BlockSpec((PEER_HEADS, tile // LANES, N_KEYS, LANES), lambda i: (0, i, 0, 0))
    table_shape = jax.ShapeDtypeStruct((PEER_HEADS, N_KEYS, t), GATE_DTYPE)
    table_spec = pl.BlockSpec((PEER_HEADS, N_KEYS, tile), lambda i: (0, 0, i))
    return pl.pallas_call(
        _peer_route_kernel,
        grid=(t // tile,),
        in_specs=[
            pl.BlockSpec((tile, d), lambda i: (i, 0)),
            pl.BlockSpec((1, d), lambda i: (0, 0)),
            pl.BlockSpec((nqd, d), lambda i: (0, 0)),
            pl.BlockSpec((PEER_HEADS * 2, N_KEYS, SUB_DIM), lambda i: (0, 0, 0)),
        ],
        out_specs=[pl.BlockSpec((d, tile), lambda i: (0, i)), scalar_spec, scalar_spec, table_spec, table_spec],
        out_shape=[jax.ShapeDtypeStruct((d, t), BF16), scalar_shape, scalar_shape, table_shape, table_shape],
        scratch_shapes=[pltpu.VMEM((PEER_HEADS * 2, N_KEYS, tile), F32)],
        compiler_params=_params("parallel"),
        name="peer_route",
    )(x2, g, wq_t, keys)


def _plane_dot(lhs_planes, rows, rhs_ref, p, after=None):
    lhs = lhs_planes[p][rows, :]
    if after is not None:
        words = pltpu.bitcast(lhs, jnp.int32)
        reps = (words.shape[0] // after.shape[0], words.shape[1] // after.shape[1])
        lhs = pltpu.bitcast(words | jnp.tile(after, reps), lhs.dtype)
    return jnp.dot(lhs, rhs_ref[p * MXU_DEPTH:(p + 1) * MXU_DEPTH, :], preferred_element_type=F32)


def _k_planes_of_transpose(table):
    k, m = table.shape
    return jnp.swapaxes(table.astype(BF16).reshape(k // MXU_DEPTH, MXU_DEPTH, m), 1, 2)


def _zero_after(v):
    bits = lax.bitcast_convert_type(v, jnp.uint32)
    return lax.bitcast_convert_type((bits >> 16) >> 16, jnp.int32)


def _replicated_row(ref, lead, row):
    return ref[(*lead, pl.ds(row, SUBLANES, stride=0), slice(None))]


def _peer_dense_kernel(final_norm, n_chunks, n_planes, xn_ref, x_ref, *refs):
    u_planes = refs[:n_planes]
    vt_ref, r1_ref, c_ref, n2_ref, w_ref, fg_ref, o_ref, acc_ref, h0_ref, h1_ref, a0_ref, a1_ref = refs[n_planes:]
    vt_planes = [vt_ref.at[p] for p in range(n_planes)]
    s = pl.program_id(0)
    tile = xn_ref.shape[1]
    n_pieces = DENSE_PIECES
    up_rows = u_planes[0].shape[0] // n_pieces
    down_rows = vt_ref.shape[1] // n_pieces
    key_rows = N_KEYS // n_pieces
    rows_per_chunk = u_planes[0].shape[0] // N_KEYS
    slot = s % 2
    down_chunk = (s - 2) % n_chunks

    @pl.when(s == 0)
    def _():
        h1_ref[...] = jnp.zeros_like(h1_ref)
        a0_ref[...] = jnp.zeros_like(a0_ref)

    @pl.when((s < 2) | (down_chunk == 0))
    def _():
        acc_ref[...] = jnp.zeros_like(acc_ref)

    def run(h_up, h_gate, a_gate, a_down):
        def piece_body(k, carry):
            up = pl.ds(pl.multiple_of(k * up_rows, up_rows), up_rows)
            down = pl.ds(pl.multiple_of(k * down_rows, down_rows), down_rows)
            slabs = [(jg, sub) for jg in range(rows_per_chunk // SLAB_I1) for sub in range(key_rows // SLAB_ROWS)]
            slabs_per_dot = len(slabs) // (2 * n_planes)
            slab = (SLAB_ROWS // BF16_ROWS, BF16_ROWS, tile)

            def scalar_rows(ref, hd, j, anchor):
                words = [_replicated_row(ref, (hd, lc), j) for lc in range(tile // LANES)]
                if anchor is not None:
                    words[-1] = words[-1] | anchor
                return jnp.concatenate([pltpu.bitcast(wd, GATE_DTYPE) for wd in words], axis=1)[None]

            def gate_slab(jg, sub, anchor):
                key0 = pl.multiple_of(k * key_rows + sub * SLAB_ROWS, SLAB_ROWS)
                keys = pl.ds(key0, SLAB_ROWS)
                i1_rows = range(jg * SLAB_I1, (jg + 1) * SLAB_I1)
                gates = [jnp.zeros(slab, GATE_DTYPE) for _ in i1_rows]
                for hd in range(PEER_HEADS):
                    n2 = n2_ref[hd, keys, :].reshape(slab)
                    w = w_ref[hd, keys, :].reshape(slab)
                    for idx, j in enumerate(i1_rows):
                        hold = anchor if (hd == PEER_HEADS - 1 and idx == SLAB_I1 - 1) else None
                        r1 = scalar_rows(r1_ref, hd, j, hold)
                        c = scalar_rows(c_ref, hd, j, None)
                        gates[idx] = gates[idx] + jnp.where(r1 < n2, w, jnp.zeros_like(w)) * c
                for idx, j in enumerate(i1_rows):
                    rows = pl.ds(pl.multiple_of(j * N_KEYS + key0, SLAB_ROWS), SLAB_ROWS)
                    h = h_gate[rows, :]
                    act = (0.5 * h * (1.0 + lax.erf(h * (1.0 / math.sqrt(2.0))))).astype(GATE_DTYPE)
                    out = (gates[idx].reshape(SLAB_ROWS, tile) * act).astype(a_gate.dtype)
                    a_gate[rows, :] = out
                return _zero_after(pltpu.bitcast(out[:, :LANES], jnp.int32))

            dots = ([("down", vt_planes, down, a_down, p) for p in range(n_planes)]
                    + [("up", u_planes, up, xn_ref, p) for p in range(n_planes)])
            sums = {}
            tokens = []
            for i, (name, lhs_planes, rows, rhs_ref, p) in enumerate(dots):
                after = tokens[i - DOT_LAG] if i >= DOT_LAG else None
                part = _plane_dot(lhs_planes, rows, rhs_ref, p, after)
                sums[name] = part if p == 0 else sums[name] + part
                if (name, p) == ("up", n_planes - 1):
                    h_up[up, :] = sums["up"]
                if (name, p) == ("down", n_planes - 1):
                    acc_ref[down, :] += sums["down"]
                for j, sub in slabs[i * slabs_per_dot:(i + 1) * slabs_per_dot]:
                    anchor = None
                    if (j, sub) == slabs[-1]:
                        anchor = (_zero_after(sums["up"][-SUBLANES:, :LANES])
                                  | _zero_after(sums["up"][-SUBLANES:, -LANES:])
                                  | _zero_after(sums["down"][-SUBLANES:, :LANES])
                                  | _zero_after(sums["down"][-SUBLANES:, -LANES:]))
                    token = gate_slab(j, sub, anchor)
                tokens.append(token)
            return carry

        lax.fori_loop(0, n_pieces, piece_body, 0)

    @pl.when(slot == 0)
    def _():
        run(h0_ref, h1_ref, a1_ref, a0_ref)

    @pl.when(slot == 1)
    def _():
        run(h1_ref, h0_ref, a0_ref, a1_ref)

    @pl.when((s >= 2) & (down_chunk == n_chunks - 1))
    def _():
        y = x_ref[...] + acc_ref[...].T
        if final_norm:
            y = _rmsnorm(y, fg_ref[...])
        o_ref[...] = y


def _peer_dense(xn, x2, u, vt, r1, c, n2, w, final_g, final_norm):
    t, d = x2.shape
    n_experts = u.shape[0]
    n_planes = d // MXU_DEPTH
    tile = min(DENSE_TILE, t)
    ec = EXPERT_CHUNK
    rows = ec // N_KEYS
    n_chunks = n_experts // ec
    n_pos = (t // tile) * n_chunks

    def pos(s, lag):
        p = jnp.clip(s - lag, 0, n_pos - 1)
        return p // n_chunks, p % n_chunks

    up_tile = lambda s: pos(s, 0)[0]
    up_chunk = lambda s: pos(s, 0)[1]
    gate_tile = lambda s: pos(s, 1)[0]
    gate_chunk = lambda s: pos(s, 1)[1]
    down_tile = lambda s: pos(s, 2)[0]
    down_chunk = lambda s: pos(s, 2)[1]
    return pl.pallas_call(
        functools.partial(_peer_dense_kernel, final_norm, n_chunks, n_planes),
        grid=(n_pos + 2,),
        in_specs=[
            pl.BlockSpec((d, tile), lambda s: (0, up_tile(s))),
            pl.BlockSpec((tile, d), lambda s: (down_tile(s), 0)),
            *[pl.BlockSpec((ec, MXU_DEPTH), functools.partial(lambda p, s: (up_chunk(s), p), p))
              for p in range(n_planes)],
            pl.BlockSpec((ec // MXU_DEPTH, d, MXU_DEPTH), lambda s: (down_chunk(s), 0, 0)),
            pl.BlockSpec((PEER_HEADS, tile // LANES, rows, LANES), lambda s: (0, gate_tile(s), gate_chunk(s), 0)),
            pl.BlockSpec((PEER_HEADS, tile // LANES, rows, LANES), lambda s: (0, gate_tile(s), gate_chunk(s), 0)),
            pl.BlockSpec((PEER_HEADS, N_KEYS, tile), lambda s: (0, 0, gate_tile(s))),
            pl.BlockSpec((PEER_HEADS, N_KEYS, tile), lambda s: (0, 0, gate_tile(s))),
            pl.BlockSpec((1, d), lambda s: (0, 0)),
        ],
        out_specs=pl.BlockSpec((tile, d), lambda s: (down_tile(s), 0)),
        out_shape=jax.ShapeDtypeStruct((t, d), F32),
        scratch_shapes=[
            pltpu.VMEM((d, tile), F32),
            pltpu.VMEM((ec, tile), F32),
            pltpu.VMEM((ec, tile), F32),
            pltpu.VMEM((ec, tile), BF16),
            pltpu.VMEM((ec, tile), BF16),
        ],
        compiler_params=_params("arbitrary"),
        name="peer_dense",
    )(xn, x2, *([u] * n_planes), vt, r1, c, n2, w, final_g)


def _peer(x2, g, w_q, subkeys, u_tab, v_tab, final_g, final_norm):
    d = x2.shape[1]
    wq_t = w_q.astype(BF16).T
    keys = subkeys.reshape(PEER_HEADS * 2, N_KEYS, SUB_DIM).astype(BF16)
    xn, r1, c, n2, w = _peer_route(x2, g.reshape(1, d), wq_t, keys)
    return _peer_dense(xn, x2, u_tab.astype(BF16), _k_planes_of_transpose(v_tab), r1, c, n2, w,
                       final_g.reshape(1, d), final_norm)


def kernel(x, conv_norm_g, conv_w_in, conv_w, conv_w_out, attn_norm_g, attn_w_qkv, attn_sink, attn_w_o,
           rel_bias, ffn_norm_g, peer_w_q, peer_subkeys, peer_u, peer_v, final_norm_g):
    bsz, seq, d = x.shape
    depth = ffn_norm_g.shape[0]
    x2 = x.reshape(bsz * seq, d)
    bucket = _t5_bucket_table()
    for i in range(depth):
        j = i // 2
        if i % 2 == 0:
            u, gb = _conv_in(x2, conv_norm_g[j].reshape(1, d), conv_w_in[j].astype(BF16))
            x2 = _conv_out(u, gb, x2, conv_w[j], conv_w_out[j].astype(BF16), seq)
        else:
            q, k, v = _attn_qkv(x2, attn_norm_g[j].reshape(1, d), attn_w_qkv[j].astype(BF16))
            x2 = _attn(q, k, v, bucket, rel_bias, attn_sink[j], x2, attn_w_o[j].astype(BF16), seq)
        x2 = _peer(x2, ffn_norm_g[i], peer_w_q[i], peer_subkeys[i], peer_u[i], peer_v[i],
                   final_norm_g, i == depth - 1)
    return x2.reshape(bsz, seq, d)
```

```python
import functools
import math

import jax
import jax.numpy as jnp
import numpy as np
from jax import lax
from jax.experimental import pallas as pl
from jax.experimental.pallas import tpu as pltpu

F32 = jnp.float32
BF16 = jnp.bfloat16
GATE_DTYPE = jnp.bfloat16

RMS_EPS = 1e-6
CONV_WIDTH = 3
N_HEADS = 16
N_KV_HEADS = 4
HEAD_DIM = 64
GROUP = N_HEADS // N_KV_HEADS
WINDOW = 128
BLOCK = 128
NEG_INF = -1e30
N_BUCKETS = 32
MAX_DISTANCE = 128
PEER_HEADS = 8
N_KEYS = 128
PEER_TOPK = 16
SUB_DIM = 128

SUBLANES = 8
LANES = 128
BF16_ROWS = 2 * SUBLANES
MXU_DEPTH = 256
VMEM_LIMIT_BYTES = 52 * 1024 * 1024

CONV_TILE = 512
QKV_TILE = 512
ROUTE_TILE = 256
ROUTE_HEADS_PER_ITER = 8
DENSE_TILE = 512
EXPERT_CHUNK = 1024
DENSE_PIECES = 2
DOT_LAG = 1
SLAB_ROWS = 32
SLAB_I1 = 2


def _params(*semantics, flags=None):
    return pltpu.CompilerParams(dimension_semantics=semantics, vmem_limit_bytes=VMEM_LIMIT_BYTES,
                                flags=flags)


def _rmsnorm(x, g):
    ms = jnp.mean(x * x, axis=-1, keepdims=True)
    return x * lax.rsqrt(ms + RMS_EPS) * g


def _conv_in_kernel(x_ref, g_ref, w_ref, u_ref, gb_ref):
    d = x_ref.shape[1]
    xn = _rmsnorm(x_ref[...], g_ref[...]).astype(BF16)
    gb_ref[...] = jnp.dot(xn, w_ref[:, :d], preferred_element_type=F32)
    gate_c = jnp.dot(xn, w_ref[:, d:2 * d], preferred_element_type=F32)
    h = jnp.dot(xn, w_ref[:, 2 * d:], preferred_element_type=F32)
    u_ref[...] = gate_c * h


def _conv_in(x2, g, w_in):
    t, d = x2.shape
    tile = min(CONV_TILE, t)
    return pl.pallas_call(
        _conv_in_kernel,
        grid=(t // tile,),
        in_specs=[
            pl.BlockSpec((tile, d), lambda i: (i, 0)),
            pl.BlockSpec((1, d), lambda i: (0, 0)),
            pl.BlockSpec((d, 3 * d), lambda i: (0, 0)),
        ],
        out_specs=[
            pl.BlockSpec((tile, d), lambda i: (i, 0)),
            pl.BlockSpec((tile, d), lambda i: (i, 0)),
        ],
        out_shape=[jax.ShapeDtypeStruct((t, d), F32), jax.ShapeDtypeStruct((t, d), F32)],
        compiler_params=_params("parallel"),
        name="conv_in",
    )(x2, g, w_in)


def _conv_out_kernel(tiles_per_seq, u_ref, uprev_ref, unext_ref, gb_ref, x_ref, cw_ref, w_ref, o_ref):
    tile = u_ref.shape[0]
    pos = pl.program_id(0) % tiles_per_seq
    u = u_ref[...]
    prev_row = jnp.where(pos == 0, 0.0, uprev_ref[SUBLANES - 1:SUBLANES, :])
    next_row = jnp.where(pos == tiles_per_seq - 1, 0.0, unext_ref[0:1, :])
    row = lax.broadcasted_iota(jnp.int32, u.shape, 0)
    u_before = jnp.where(row == 0, prev_row, pltpu.roll(u, 1, axis=0))
    u_after = jnp.where(row == tile - 1, next_row, pltpu.roll(u, tile - 1, axis=0))
    y = cw_ref[0:1, :] * u_before + cw_ref[1:2, :] * u + cw_ref[2:3, :] * u_after
    z = (gb_ref[...] * y).astype(BF16)
    o_ref[...] = x_ref[...] + jnp.dot(z, w_ref[...], preferred_element_type=F32)


def _conv_out(u, gb, x2, conv_w, w_out, seq):
    t, d = x2.shape
    tile = min(CONV_TILE, seq)
    halo_per_tile = tile // SUBLANES
    n_halo = t // SUBLANES
    return pl.pallas_call(
        functools.partial(_conv_out_kernel, seq // tile),
        grid=(t // tile,),
        in_specs=[
            pl.BlockSpec((tile, d), lambda i: (i, 0)),
            pl.BlockSpec((SUBLANES, d), lambda i: (jnp.maximum(i * halo_per_tile - 1, 0), 0)),
            pl.BlockSpec((SUBLANES, d), lambda i: (jnp.minimum((i + 1) * halo_per_tile, n_halo - 1), 0)),
            pl.BlockSpec((tile, d), lambda i: (i, 0)),
            pl.BlockSpec((tile, d), lambda i: (i, 0)),
            pl.BlockSpec((CONV_WIDTH, d), lambda i: (0, 0)),
            pl.BlockSpec((d, d), lambda i: (0, 0)),
        ],
        out_specs=pl.BlockSpec((tile, d), lambda i: (i, 0)),
        out_shape=jax.ShapeDtypeStruct((t, d), F32),
        compiler_params=_params("parallel"),
        name="conv_out",
    )(u, u, u, gb, x2, conv_w, w_out)


def _attn_qkv_kernel(x_ref, g_ref, w_ref, q_ref, k_ref, v_ref):
    nq = q_ref.shape[1]
    nk = k_ref.shape[1]
    xn = _rmsnorm(x_ref[...], g_ref[...]).astype(BF16)
    q = jnp.dot(xn, w_ref[:, :nq], preferred_element_type=F32) * (1.0 / math.sqrt(HEAD_DIM))
    q_ref[...] = q.astype(BF16)
    k_ref[...] = jnp.dot(xn, w_ref[:, nq:nq + nk], preferred_element_type=F32).astype(BF16)
    v_ref[...] = jnp.dot(xn, w_ref[:, nq + nk:], preferred_element_type=F32).astype(BF16)


def _attn_qkv(x2, g, w_qkv):
    t, d = x2.shape
    tile = min(QKV_TILE, t)
    nq = N_HEADS * HEAD_DIM
    nk = N_KV_HEADS * HEAD_DIM
    return pl.pallas_call(
        _attn_qkv_kernel,
        grid=(t // tile,),
        in_specs=[
            pl.BlockSpec((tile, d), lambda i: (i, 0)),
            pl.BlockSpec((1, d), lambda i: (0, 0)),
            pl.BlockSpec((d, nq + 2 * nk), lambda i: (0, 0)),
        ],
        out_specs=[
            pl.BlockSpec((tile, nq), lambda i: (i, 0)),
            pl.BlockSpec((tile, nk), lambda i: (i, 0)),
            pl.BlockSpec((tile, nk), lambda i: (i, 0)),
        ],
        out_shape=[
            jax.ShapeDtypeStruct((t, nq), BF16),
            jax.ShapeDtypeStruct((t, nk), BF16),
            jax.ShapeDtypeStruct((t, nk), BF16),
        ],
        compiler_params=_params("parallel"),
        name="attn_qkv",
    )(x2, g, w_qkv)


def _attn_kernel(blocks_per_seq, q_ref, kp_ref, kc_ref, kn_ref, vp_ref, vc_ref, vn_ref, bucket_ref,
                 relb_ref, sink_ref, x_ref, wo_ref, o_ref, bias_ref, att_ref):
    step = pl.program_id(0)

    @pl.when(step == 0)
    def _():
        bucket = bucket_ref[...]
        rel = lax.broadcasted_iota(jnp.int32, bucket.shape, 1) - BLOCK - lax.broadcasted_iota(
            jnp.int32, bucket.shape, 0)
        in_window = jnp.abs(rel) <= WINDOW

        def head_body(hq, carry):
            def bucket_body(b, acc):
                return jnp.where(bucket == b, relb_ref[b, hq], acc)

            table = lax.fori_loop(0, N_BUCKETS, bucket_body, jnp.zeros(bucket.shape, F32))
            bias_ref[hq] = jnp.where(in_window, table, NEG_INF)
            return carry

        lax.fori_loop(0, N_HEADS, head_body, 0)

    pos = step % blocks_per_seq
    has_prev = pos > 0
    has_next = pos < blocks_per_seq - 1
    col = lax.broadcasted_iota(jnp.int32, (BLOCK, 3 * BLOCK), 1)
    key_ok = ((col >= BLOCK) | has_prev) & ((col < 2 * BLOCK) | has_next)

    for kvh in range(N_KV_HEADS):
        lo = kvh * HEAD_DIM
        k = jnp.concatenate([kp_ref[:, lo:lo + HEAD_DIM], kc_ref[:, lo:lo + HEAD_DIM],
                             kn_ref[:, lo:lo + HEAD_DIM]], axis=0)
        v = jnp.concatenate([vp_ref[:, lo:lo + HEAD_DIM], vc_ref[:, lo:lo + HEAD_DIM],
                             vn_ref[:, lo:lo + HEAD_DIM]], axis=0)
        for g in range(GROUP):
            hq = kvh * GROUP + g
            q = q_ref[:, hq * HEAD_DIM:(hq + 1) * HEAD_DIM]
            s = lax.dot_general(q, k, (((1,), (1,)), ((), ())), preferred_element_type=F32)
            logits = jnp.where(key_ok, s + bias_ref[hq], NEG_INF)
            sink = sink_ref[hq]
            m = jnp.maximum(jnp.max(logits, axis=-1, keepdims=True), sink)
            p = jnp.exp(logits - m)
            den = jnp.sum(p, axis=-1, keepdims=True) + jnp.exp(sink - m)
            o = jnp.dot(p.astype(BF16), v, preferred_element_type=F32) / den
            att_ref[:, hq * HEAD_DIM:(hq + 1) * HEAD_DIM] = o.astype(BF16)

    o_ref[...] = x_ref[...] + jnp.dot(att_ref[...], wo_ref[...], preferred_element_type=F32)


def _attn(q, k, v, bucket, rel_bias, sink, x2, w_o, seq):
    t, d = x2.shape
    nq = N_HEADS * HEAD_DIM
    nk = N_KV_HEADS * HEAD_DIM
    n_blocks = t // BLOCK
    prev_map = lambda i: (jnp.maximum(i - 1, 0), 0)
    cur_map = lambda i: (i, 0)
    next_map = lambda i: (jnp.minimum(i + 1, n_blocks - 1), 0)
    smem = pl.BlockSpec(memory_space=pltpu.SMEM)
    return pl.pallas_call(
        functools.partial(_attn_kernel, seq // BLOCK),
        grid=(n_blocks,),
        in_specs=[
            pl.BlockSpec((BLOCK, nq), cur_map),
            pl.BlockSpec((BLOCK, nk), prev_map),
            pl.BlockSpec((BLOCK, nk), cur_map),
            pl.BlockSpec((BLOCK, nk), next_map),
            pl.BlockSpec((BLOCK, nk), prev_map),
            pl.BlockSpec((BLOCK, nk), cur_map),
            pl.BlockSpec((BLOCK, nk), next_map),
            pl.BlockSpec((BLOCK, 3 * BLOCK), lambda i: (0, 0)),
            smem,
            smem,
            pl.BlockSpec((BLOCK, d), cur_map),
            pl.BlockSpec((nq, d), lambda i: (0, 0)),
        ],
        out_specs=pl.BlockSpec((BLOCK, d), cur_map),
        out_shape=jax.ShapeDtypeStruct((t, d), F32),
        scratch_shapes=[
            pltpu.VMEM((N_HEADS, BLOCK, 3 * BLOCK), F32),
            pltpu.VMEM((BLOCK, nq), BF16),
        ],
        compiler_params=_params("arbitrary"),
        name="attn",
    )(q, k, k, k, v, v, v, bucket, rel_bias, sink, x2, w_o)


def _t5_bucket_table():
    half = N_BUCKETS // 2
    max_exact = half // 2
    qi = jnp.arange(BLOCK)[:, None]
    kj = jnp.arange(3 * BLOCK)[None, :]
    rel = kj - BLOCK - qi
    ret = jnp.where(rel > 0, half, 0)
    n = jnp.abs(rel)
    nf = jnp.maximum(n, 1).astype(jnp.float32)
    large = max_exact + (jnp.log(nf / max_exact) / math.log(MAX_DISTANCE / max_exact)
                         * (half - max_exact)).astype(jnp.int32)
    large = jnp.minimum(large, half - 1)
    return (ret + jnp.where(n < max_exact, n, large)).astype(jnp.int32)


INT32_MIN = -(2 ** 31)


def _ordered_int(bits):
    return bits ^ ((bits >> 31) & 0x7FFFFFFF)


def _top16_ranks(s, key_iota):
    bits = lax.bitcast_convert_type(s, jnp.int32)
    keys = _ordered_int(jnp.where(bits == INT32_MIN, 0, bits))
    slot = lax.broadcasted_iota(jnp.int32, (PEER_TOPK, s.shape[1]), 0)
    top = jnp.zeros((PEER_TOPK, s.shape[1]), jnp.int32)
    for r in range(PEER_TOPK):
        m = jnp.max(keys, axis=0, keepdims=True)
        first = jnp.min(jnp.where(keys == m, key_iota, float(N_KEYS)), axis=0, keepdims=True)
        keys = jnp.where(key_iota == first, INT32_MIN + r, keys)
        top = jnp.where(slot == r, m, top)
    rank = jnp.where(keys < INT32_MIN + PEER_TOPK, keys ^ INT32_MIN, PEER_TOPK).astype(F32)
    return rank, lax.bitcast_convert_type(_ordered_int(top), F32)


def _staircase(sv1, sv2):
    lanes = sv1.shape[1]
    slot = lax.broadcasted_iota(jnp.int32, (PEER_TOPK, lanes), 0).astype(F32)
    n = jnp.zeros((PEER_TOPK, lanes), F32)
    front = sv1[0:1, :] + sv2
    big = float(PEER_TOPK * PEER_TOPK + PEER_TOPK)
    for r in range(PEER_TOPK):
        m = jnp.max(front, axis=0, keepdims=True)
        flat = n * float(PEER_TOPK) + slot
        first = jnp.min(jnp.where(front == m, flat, big), axis=0, keepdims=True)
        chosen = flat == first
        n = jnp.where(chosen, n + 1.0, n)
        if r + 1 < PEER_TOPK:
            depth = jnp.sum(jnp.where(chosen, n, 0.0), axis=0, keepdims=True)
            nxt = jnp.sum(jnp.where(slot == depth, sv1, 0.0), axis=0, keepdims=True)
            front = jnp.where(chosen, jnp.where(depth < float(PEER_TOPK), nxt + sv2, -jnp.inf), front)
    return n


def _twice_bf16(v):
    bits = lax.bitcast_convert_type(v.astype(GATE_DTYPE).astype(F32), jnp.uint32)
    return lax.bitcast_convert_type(bits | (bits >> 16), jnp.int32)


def _peer_route_kernel(x_ref, g_ref, wq_ref, keys_ref, xn_ref, r1_ref, c_ref, n2_ref, w_ref, s_ref):
    tile = x_ref.shape[0]
    xn_t = _rmsnorm(x_ref[...], g_ref[...]).T.astype(BF16)
    xn_ref[...] = xn_t
    q_t = jnp.dot(wq_ref[...], xn_t, preferred_element_type=F32)
    for hc in range(PEER_HEADS * 2):
        q_hc = q_t[hc * SUB_DIM:(hc + 1) * SUB_DIM, :].astype(BF16)
        s_ref[hc] = jnp.dot(keys_ref[hc], q_hc, preferred_element_type=F32)

    key_iota = lax.broadcasted_iota(jnp.int32, (N_KEYS, LANES), 0).astype(F32)
    slot = lax.broadcasted_iota(jnp.int32, (PEER_TOPK, LANES), 0).astype(F32)

    def route_head(hd, lane_chunk):
        lane0 = pl.multiple_of(lane_chunk * LANES, LANES)
        lanes = pl.ds(lane0, LANES)
        s1 = s_ref[2 * hd, :, lanes]
        s2 = s_ref[2 * hd + 1, :, lanes]
        rank1, sv1 = _top16_ranks(s1, key_iota)
        rank2, sv2 = _top16_ranks(s2, key_iota)
        n = _staircase(sv1, sv2)
        e1 = jnp.exp(s1 - sv1[0:1, :])
        e2 = jnp.exp(s2 - sv2[0:1, :])
        e1s = jnp.exp(sv1 - sv1[0:1, :])
        e2s = jnp.exp(sv2 - sv2[0:1, :])
        inner = jnp.zeros_like(e2s)
        for p in range(PEER_TOPK):
            inner = inner + jnp.where(n > float(p), e1s[p:p + 1, :], 0.0)
        z = jnp.sum(inner * e2s, axis=0, keepdims=True)
        n2 = jnp.zeros_like(rank2)
        for q in range(PEER_TOPK):
            n2 = jnp.where(rank2 == float(q), n[q:q + 1, :], n2)
        r1_ref[hd, lane_chunk] = _twice_bf16(rank1)
        c_ref[hd, lane_chunk] = _twice_bf16(e1 / z)
        n2_ref[hd, :, lanes] = n2.astype(GATE_DTYPE)
        w_ref[hd, :, lanes] = e2.astype(GATE_DTYPE)

    def body(it, carry):
        lane_chunk = it % (tile // LANES)
        for i in range(ROUTE_HEADS_PER_ITER):
            route_head((it // (tile // LANES)) * ROUTE_HEADS_PER_ITER + i, lane_chunk)
        return carry

    lax.fori_loop(0, (PEER_HEADS // ROUTE_HEADS_PER_ITER) * (tile // LANES), body, 0)


def _peer_route(x2, g, wq_t, keys):
    t, d = x2.shape
    tile = min(ROUTE_TILE, t)
    nqd = PEER_HEADS * 2 * SUB_DIM
    scalar_shape = jax.ShapeDtypeStruct((PEER_HEADS, t // LANES, N_KEYS, LANES), jnp.int32)
    scalar_spec = pl.BlockSpec((PEER_HEADS, tile // LANES, N_KEYS, LANES), lambda i: (0, i, 0, 0))
    table_shape = jax.ShapeDtypeStruct((PEER_HEADS, N_KEYS, t), GATE_DTYPE)
    table_spec = pl.BlockSpec((PEER_HEADS, N_KEYS, tile), lambda i: (0, 0, i))
    return pl.pallas_call(
        _peer_route_kernel,
        grid=(t // tile,),
        in_specs=[
            pl.BlockSpec((tile, d), lambda i: (i, 0)),
            pl.BlockSpec((1, d), lambda i: (0, 0)),
            pl.BlockSpec((nqd, d), lambda i: (0, 0)),
            pl.BlockSpec((PEER_HEADS * 2, N_KEYS, SUB_DIM), lambda i: (0, 0, 0)),
        ],
        out_specs=[pl.BlockSpec((d, tile), lambda i: (0, i)), scalar_spec, scalar_spec, table_spec, table_spec],
        out_shape=[jax.ShapeDtypeStruct((d, t), BF16), scalar_shape, scalar_shape, table_shape, table_shape],
        scratch_shapes=[pltpu.VMEM((PEER_HEADS * 2, N_KEYS, tile), F32)],
        compiler_params=_params("parallel"),
        name="peer_route",
    )(x2, g, wq_t, keys)


def _plane_dot(lhs_planes, rows, rhs_ref, p, after=None):
    lhs = lhs_planes[p][rows, :]
    if after is not None:
        words = pltpu.bitcast(lhs, jnp.int32)
        reps = (words.shape[0] // after.shape[0], words.shape[1] // after.shape[1])
        lhs = pltpu.bitcast(words | jnp.tile(after, reps), lhs.dtype)
    return jnp.dot(lhs, rhs_ref[p * MXU_DEPTH:(p + 1) * MXU_DEPTH, :], preferred_element_type=F32)


def _k_planes_of_transpose(table):
    k, m = table.shape
    return jnp.swapaxes(table.astype(BF16).reshape(k // MXU_DEPTH, MXU_DEPTH, m), 1, 2)


def _zero_after(v):
    bits = lax.bitcast_convert_type(v, jnp.uint32)
    return lax.bitcast_convert_type((bits >> 16) >> 16, jnp.int32)


def _replicated_row(ref, lead, row):
    return ref[(*lead, pl.ds(row, SUBLANES, stride=0), slice(None))]


def _peer_dense_kernel(final_norm, n_chunks, n_planes, xn_ref, x_ref, *refs):
    u_planes = refs[:n_planes]
    vt_ref, r1_ref, c_ref, n2_ref, w_ref, fg_ref, o_ref, acc_ref, h0_ref, h1_ref, a0_ref, a1_ref = refs[n_planes:]
    vt_planes = [vt_ref.at[p] for p in range(n_planes)]
    s = pl.program_id(0)
    tile = xn_ref.shape[1]
    n_pieces = DENSE_PIECES
    up_rows = u_planes[0].shape[0] // n_pieces
    down_rows = vt_ref.shape[1] // n_pieces
    key_rows = N_KEYS // n_pieces
    rows_per_chunk = u_planes[0].shape[0] // N_KEYS
    slot = s % 2
    down_chunk = (s - 2) % n_chunks

    @pl.when(s == 0)
    def _():
        h1_ref[...] = jnp.zeros_like(h1_ref)
        a0_ref[...] = jnp.zeros_like(a0_ref)

    @pl.when((s < 2) | (down_chunk == 0))
    def _():
        acc_ref[...] = jnp.zeros_like(acc_ref)

    def run(h_up, h_gate, a_gate, a_down):
        def piece_body(k, carry):
            up = pl.ds(pl.multiple_of(k * up_rows, up_rows), up_rows)
            down = pl.ds(pl.multiple_of(k * down_rows, down_rows), down_rows)
            slabs = [(jg, sub) for jg in range(rows_per_chunk // SLAB_I1) for sub in range(key_rows // SLAB_ROWS)]
            slabs_per_dot = len(slabs) // (2 * n_planes)
            slab = (SLAB_ROWS // BF16_ROWS, BF16_ROWS, tile)

            def scalar_rows(ref, hd, j, anchor):
                words = [_replicated_row(ref, (hd, lc), j) for lc in range(tile // LANES)]
                if anchor is not None:
                    words[-1] = words[-1] | anchor
                return jnp.concatenate([pltpu.bitcast(wd, GATE_DTYPE) for wd in words], axis=1)[None]

            def gate_slab(jg, sub, anchor):
                key0 = pl.multiple_of(k * key_rows + sub * SLAB_ROWS, SLAB_ROWS)
                keys = pl.ds(key0, SLAB_ROWS)
                i1_rows = range(jg * SLAB_I1, (jg + 1) * SLAB_I1)
                gates = [jnp.zeros(slab, GATE_DTYPE) for _ in i1_rows]
                for hd in range(PEER_HEADS):
                    n2 = n2_ref[hd, keys, :].reshape(slab)
                    w = w_ref[hd, keys, :].reshape(slab)
                    for idx, j in enumerate(i1_rows):
                        hold = anchor if (hd == PEER_HEADS - 1 and idx == SLAB_I1 - 1) else None
                        r1 = scalar_rows(r1_ref, hd, j, hold)
                        c = scalar_rows(c_ref, hd, j, None)
                        gates[idx] = gates[idx] + jnp.where(r1 < n2, w, jnp.zeros_like(w)) * c
                for idx, j in enumerate(i1_rows):
                    rows = pl.ds(pl.multiple_of(j * N_KEYS + key0, SLAB_ROWS), SLAB_ROWS)
                    h = h_gate[rows, :]
                    act = (0.5 * h * (1.0 + lax.erf(h * (1.0 / math.sqrt(2.0))))).astype(GATE_DTYPE)
                    out = (gates[idx].reshape(SLAB_ROWS, tile) * act).astype(a_gate.dtype)
                    a_gate[rows, :] = out
                return _zero_after(pltpu.bitcast(out[:, :LANES], jnp.int32))

            dots = ([("down", vt_planes, down, a_down, p) for p in range(n_planes)]
                    + [("up", u_planes, up, xn_ref, p) for p in range(n_planes)])
            sums = {}
            tokens = []
            for i, (name, lhs_planes, rows, rhs_ref, p) in enumerate(dots):
                after = tokens[i - DOT_LAG] if i >= DOT_LAG else None
                part = _plane_dot(lhs_planes, rows, rhs_ref, p, after)
                sums[name] = part if p == 0 else sums[name] + part
                if (name, p) == ("up", n_planes - 1):
                    h_up[up, :] = sums["up"]
                if (name, p) == ("down", n_planes - 1):
                    acc_ref[down, :] += sums["down"]
                for j, sub in slabs[i * slabs_per_dot:(i + 1) * slabs_per_dot]:
                    anchor = None
                    if (j, sub) == slabs[-1]:
                        anchor = (_zero_after(sums["up"][-SUBLANES:, :LANES])
                                  | _zero_after(sums["up"][-SUBLANES:, -LANES:])
                                  | _zero_after(sums["down"][-SUBLANES:, :LANES])
                                  | _zero_after(sums["down"][-SUBLANES:, -LANES:]))
                    token = gate_slab(j, sub, anchor)
                tokens.append(token)
            return carry

        lax.fori_loop(0, n_pieces, piece_body, 0)

    @pl.when(slot == 0)
    def _():
        run(h0_ref, h1_ref, a1_ref, a0_ref)

    @pl.when(slot == 1)
    def _():
        run(h1_ref, h0_ref, a0_ref, a1_ref)

    @pl.when((s >= 2) & (down_chunk == n_chunks - 1))
    def _():
        y = x_ref[...] + acc_ref[...].T
        if final_norm:
            y = _rmsnorm(y, fg_ref[...])
        o_ref[...] = y


def _peer_dense(xn, x2, u, vt, r1, c, n2, w, final_g, final_norm):
    t, d = x2.shape
    n_experts = u.shape[0]
    n_planes = d // MXU_DEPTH
    tile = min(DENSE_TILE, t)
    ec = EXPERT_CHUNK
    rows = ec // N_KEYS
    n_chunks = n_experts // ec
    n_pos = (t // tile) * n_chunks

    def pos(s, lag):
        p = jnp.clip(s - lag, 0, n_pos - 1)
        return p // n_chunks, p % n_chunks

    up_tile = lambda s: pos(s, 0)[0]
    up_chunk = lambda s: pos(s, 0)[1]
    gate_tile = lambda s: pos(s, 1)[0]
    gate_chunk = lambda s: pos(s, 1)[1]
    down_tile = lambda s: pos(s, 2)[0]
    down_chunk = lambda s: pos(s, 2)[1]
    return pl.pallas_call(
        functools.partial(_peer_dense_kernel, final_norm, n_chunks, n_planes),
        grid=(n_pos + 2,),
        in_specs=[
            pl.BlockSpec((d, tile), lambda s: (0, up_tile(s))),
            pl.BlockSpec((tile, d), lambda s: (down_tile(s), 0)),
            *[pl.BlockSpec((ec, MXU_DEPTH), functools.partial(lambda p, s: (up_chunk(s), p), p))
              for p in range(n_planes)],
            pl.BlockSpec((ec // MXU_DEPTH, d, MXU_DEPTH), lambda s: (down_chunk(s), 0, 0)),
            pl.BlockSpec((PEER_HEADS, tile // LANES, rows, LANES), lambda s: (0, gate_tile(s), gate_chunk(s), 0)),
            pl.BlockSpec((PEER_HEADS, tile // LANES, rows, LANES), lambda s: (0, gate_tile(s), gate_chunk(s), 0)),
            pl.BlockSpec((PEER_HEADS, N_KEYS, tile), lambda s: (0, 0, gate_tile(s))),
            pl.BlockSpec((PEER_HEADS, N_KEYS, tile), lambda s: (0, 0, gate_tile(s))),
            pl.BlockSpec((1, d), lambda s: (0, 0)),
        ],
        out_specs=pl.BlockSpec((tile, d), lambda s: (down_tile(s), 0)),
        out_shape=jax.ShapeDtypeStruct((t, d), F32),
        scratch_shapes=[
            pltpu.VMEM((d, tile), F32),
            pltpu.VMEM((ec, tile), F32),
            pltpu.VMEM((ec, tile), F32),
            pltpu.VMEM((ec, tile), BF16),
            pltpu.VMEM((ec, tile), BF16),
        ],
        compiler_params=_params("arbitrary"),
        name="peer_dense",
    )(xn, x2, *([u] * n_planes), vt, r1, c, n2, w, final_g)


def _peer(x2, g, w_q, subkeys, u_tab, v_tab, final_g, final_norm):
    d = x2.shape[1]
    wq_t = w_q.astype(BF16).T
    keys = subkeys.reshape(PEER_HEADS * 2, N_KEYS, SUB_DIM).astype(BF16)
    xn, r1, c, n2, w = _peer_route(x2, g.reshape(1, d), wq_t, keys)
    return _peer_dense(xn, x2, u_tab.astype(BF16), _k_planes_of_transpose(v_tab), r1, c, n2, w,
                       final_g.reshape(1, d), final_norm)


def kernel(x, conv_norm_g, conv_w_in, conv_w, conv_w_out, attn_norm_g, attn_w_qkv, attn_sink, attn_w_o,
           rel_bias, ffn_norm_g, peer_w_q, peer_subkeys, peer_u, peer_v, final_norm_g):
    bsz, seq, d = x.shape
    depth = ffn_norm_g.shape[0]
    x2 = x.reshape(bsz * seq, d)
    bucket = _t5_bucket_table()
    for i in range(depth):
        j = i // 2
        if i % 2 == 0:
            u, gb = _conv_in(x2, conv_norm_g[j].reshape(1, d), conv_w_in[j].astype(BF16))
            x2 = _conv_out(u, gb, x2, conv_w[j], conv_w_out[j].astype(BF16), seq)
        else:
            q, k, v = _attn_qkv(x2, attn_norm_g[j].reshape(1, d), attn_w_qkv[j].astype(BF16))
            x2 = _attn(q, k, v, bucket, rel_bias, attn_sink[j], x2, attn_w_o[j].astype(BF16), seq)
        x2 = _peer(x2, ffn_norm_g[i], peer_w_q[i], peer_subkeys[i], peer_u[i], peer_v[i],
                   final_norm_g, i == depth - 1)
    return x2.reshape(bsz, seq, d)
```

```python
import functools
import math

import jax
import jax.numpy as jnp
import numpy as np
from jax import lax
from jax.experimental import pallas as pl
from jax.experimental.pallas import tpu as pltpu

F32 = jnp.float32
BF16 = jnp.bfloat16
GATE_DTYPE = jnp.bfloat16

RMS_EPS = 1e-6
CONV_WIDTH = 3
N_HEADS = 16
N_KV_HEADS = 4
HEAD_DIM = 64
GROUP = N_HEADS // N_KV_HEADS
WINDOW = 128
BLOCK = 128
NEG_INF = -1e30
N_BUCKETS = 32
MAX_DISTANCE = 128
PEER_HEADS = 8
N_KEYS = 128
PEER_TOPK = 16
SUB_DIM = 128

SUBLANES = 8
LANES = 128
BF16_ROWS = 2 * SUBLANES
MXU_DEPTH = 256
VMEM_LIMIT_BYTES = 52 * 1024 * 1024

CONV_TILE = 512
QKV_TILE = 512
ROUTE_TILE = 256
ROUTE_HEADS_PER_ITER = 8
DENSE_TILE = 512
EXPERT_CHUNK = 2048
DENSE_PIECES = 2
DOT_LAG = 1
SLAB_ROWS = 32
SLAB_I1 = 2


def _params(*semantics, flags=None):
    return pltpu.CompilerParams(dimension_semantics=semantics, vmem_limit_bytes=VMEM_LIMIT_BYTES,
                                flags=flags)


def _rmsnorm(x, g):
    ms = jnp.mean(x * x, axis=-1, keepdims=True)
    return x * lax.rsqrt(ms + RMS_EPS) * g


def _conv_in_kernel(x_ref, g_ref, w_ref, u_ref, gb_ref):
    d = x_ref.shape[1]
    xn = _rmsnorm(x_ref[...], g_ref[...]).astype(BF16)
    gb_ref[...] = jnp.dot(xn, w_ref[:, :d], preferred_element_type=F32)
    gate_c = jnp.dot(xn, w_ref[:, d:2 * d], preferred_element_type=F32)
    h = jnp.dot(xn, w_ref[:, 2 * d:], preferred_element_type=F32)
    u_ref[...] = gate_c * h


def _conv_in(x2, g, w_in):
    t, d = x2.shape
    tile = min(CONV_TILE, t)
    return pl.pallas_call(
        _conv_in_kernel,
        grid=(t // tile,),
        in_specs=[
            pl.BlockSpec((tile, d), lambda i: (i, 0)),
            pl.BlockSpec((1, d), lambda i: (0, 0)),
            pl.BlockSpec((d, 3 * d), lambda i: (0, 0)),
        ],
        out_specs=[
            pl.BlockSpec((tile, d), lambda i: (i, 0)),
            pl.BlockSpec((tile, d), lambda i: (i, 0)),
        ],
        out_shape=[jax.ShapeDtypeStruct((t, d), F32), jax.ShapeDtypeStruct((t, d), F32)],
        compiler_params=_params("parallel"),
        name="conv_in",
    )(x2, g, w_in)


def _conv_out_kernel(tiles_per_seq, u_ref, uprev_ref, unext_ref, gb_ref, x_ref, cw_ref, w_ref, o_ref):
    tile = u_ref.shape[0]
    pos = pl.program_id(0) % tiles_per_seq
    u = u_ref[...]
    prev_row = jnp.where(pos == 0, 0.0, uprev_ref[SUBLANES - 1:SUBLANES, :])
    next_row = jnp.where(pos == tiles_per_seq - 1, 0.0, unext_ref[0:1, :])
    row = lax.broadcasted_iota(jnp.int32, u.shape, 0)
    u_before = jnp.where(row == 0, prev_row, pltpu.roll(u, 1, axis=0))
    u_after = jnp.where(row == tile - 1, next_row, pltpu.roll(u, tile - 1, axis=0))
    y = cw_ref[0:1, :] * u_before + cw_ref[1:2, :] * u + cw_ref[2:3, :] * u_after
    z = (gb_ref[...] * y).astype(BF16)
    o_ref[...] = x_ref[...] + jnp.dot(z, w_ref[...], preferred_element_type=F32)


def _conv_out(u, gb, x2, conv_w, w_out, seq):
    t, d = x2.shape
    tile = min(CONV_TILE, seq)
    halo_per_tile = tile // SUBLANES
    n_halo = t // SUBLANES
    return pl.pallas_call(
        functools.partial(_conv_out_kernel, seq // tile),
        grid=(t // tile,),
        in_specs=[
            pl.BlockSpec((tile, d), lambda i: (i, 0)),
            pl.BlockSpec((SUBLANES, d), lambda i: (jnp.maximum(i * halo_per_tile - 1, 0), 0)),
            pl.BlockSpec((SUBLANES, d), lambda i: (jnp.minimum((i + 1) * halo_per_tile, n_halo - 1), 0)),
            pl.BlockSpec((tile, d), lambda i: (i, 0)),
            pl.BlockSpec((tile, d), lambda i: (i, 0)),
            pl.BlockSpec((CONV_WIDTH, d), lambda i: (0, 0)),
            pl.BlockSpec((d, d), lambda i: (0, 0)),
        ],
        out_specs=pl.BlockSpec((tile, d), lambda i: (i, 0)),
        out_shape=jax.ShapeDtypeStruct((t, d), F32),
        compiler_params=_params("parallel"),
        name="conv_out",
    )(u, u, u, gb, x2, conv_w, w_out)


def _attn_qkv_kernel(x_ref, g_ref, w_ref, q_ref, k_ref, v_ref):
    nq = q_ref.shape[1]
    nk = k_ref.shape[1]
    xn = _rmsnorm(x_ref[...], g_ref[...]).astype(BF16)
    q = jnp.dot(xn, w_ref[:, :nq], preferred_element_type=F32) * (1.0 / math.sqrt(HEAD_DIM))
    q_ref[...] = q.astype(BF16)
    k_ref[...] = jnp.dot(xn, w_ref[:, nq:nq + nk], preferred_element_type=F32).astype(BF16)
    v_ref[...] = jnp.dot(xn, w_ref[:, nq + nk:], preferred_element_type=F32).astype(BF16)


def _attn_qkv(x2, g, w_qkv):
    t, d = x2.shape
    tile = min(QKV_TILE, t)
    nq = N_HEADS * HEAD_DIM
    nk = N_KV_HEADS * HEAD_DIM
    return pl.pallas_call(
        _attn_qkv_kernel,
        grid=(t // tile,),
        in_specs=[
            pl.BlockSpec((tile, d), lambda i: (i, 0)),
            pl.BlockSpec((1, d), lambda i: (0, 0)),
            pl.BlockSpec((d, nq + 2 * nk), lambda i: (0, 0)),
        ],
        out_specs=[
            pl.BlockSpec((tile, nq), lambda i: (i, 0)),
            pl.BlockSpec((tile, nk), lambda i: (i, 0)),
            pl.BlockSpec((tile, nk), lambda i: (i, 0)),
        ],
        out_shape=[
            jax.ShapeDtypeStruct((t, nq), BF16),
            jax.ShapeDtypeStruct((t, nk), BF16),
            jax.ShapeDtypeStruct((t, nk), BF16),
        ],
        compiler_params=_params("parallel"),
        name="attn_qkv",
    )(x2, g, w_qkv)


def _attn_kernel(blocks_per_seq, q_ref, kp_ref, kc_ref, kn_ref, vp_ref, vc_ref, vn_ref, bucket_ref,
                 relb_ref, sink_ref, x_ref, wo_ref, o_ref, bias_ref, att_ref):
    step = pl.program_id(0)

    @pl.when(step == 0)
    def _():
        bucket = bucket_ref[...]
        rel = lax.broadcasted_iota(jnp.int32, bucket.shape, 1) - BLOCK - lax.broadcasted_iota(
            jnp.int32, bucket.shape, 0)
        in_window = jnp.abs(rel) <= WINDOW

        def head_body(hq, carry):
            def bucket_body(b, acc):
                return jnp.where(bucket == b, relb_ref[b, hq], acc)

            table = lax.fori_loop(0, N_BUCKETS, bucket_body, jnp.zeros(bucket.shape, F32))
            bias_ref[hq] = jnp.where(in_window, table, NEG_INF)
            return carry

        lax.fori_loop(0, N_HEADS, head_body, 0)

    pos = step % blocks_per_seq
    has_prev = pos > 0
    has_next = pos < blocks_per_seq - 1
    col = lax.broadcasted_iota(jnp.int32, (BLOCK, 3 * BLOCK), 1)
    key_ok = ((col >= BLOCK) | has_prev) & ((col < 2 * BLOCK) | has_next)

    for kvh in range(N_KV_HEADS):
        lo = kvh * HEAD_DIM
        k = jnp.concatenate([kp_ref[:, lo:lo + HEAD_DIM], kc_ref[:, lo:lo + HEAD_DIM],
                             kn_ref[:, lo:lo + HEAD_DIM]], axis=0)
        v = jnp.concatenate([vp_ref[:, lo:lo + HEAD_DIM], vc_ref[:, lo:lo + HEAD_DIM],
                             vn_ref[:, lo:lo + HEAD_DIM]], axis=0)
        for g in range(GROUP):
            hq = kvh * GROUP + g
            q = q_ref[:, hq * HEAD_DIM:(hq + 1) * HEAD_DIM]
            s = lax.dot_general(q, k, (((1,), (1,)), ((), ())), preferred_element_type=F32)
            logits = jnp.where(key_ok, s + bias_ref[hq], NEG_INF)
            sink = sink_ref[hq]
            m = jnp.maximum(jnp.max(logits, axis=-1, keepdims=True), sink)
            p = jnp.exp(logits - m)
            den = jnp.sum(p, axis=-1, keepdims=True) + jnp.exp(sink - m)
            o = jnp.dot(p.astype(BF16), v, preferred_element_type=F32) / den
            att_ref[:, hq * HEAD_DIM:(hq + 1) * HEAD_DIM] = o.astype(BF16)

    o_ref[...] = x_ref[...] + jnp.dot(att_ref[...], wo_ref[...], preferred_element_type=F32)


def _attn(q, k, v, bucket, rel_bias, sink, x2, w_o, seq):
    t, d = x2.shape
    nq = N_HEADS * HEAD_DIM
    nk = N_KV_HEADS * HEAD_DIM
    n_blocks = t // BLOCK
    prev_map = lambda i: (jnp.maximum(i - 1, 0), 0)
    cur_map = lambda i: (i, 0)
    next_map = lambda i: (jnp.minimum(i + 1, n_blocks - 1), 0)
    smem = pl.BlockSpec(memory_space=pltpu.SMEM)
    return pl.pallas_call(
        functools.partial(_attn_kernel, seq // BLOCK),
        grid=(n_blocks,),
        in_specs=[
            pl.BlockSpec((BLOCK, nq), cur_map),
            pl.BlockSpec((BLOCK, nk), prev_map),
            pl.BlockSpec((BLOCK, nk), cur_map),
            pl.BlockSpec((BLOCK, nk), next_map),
            pl.BlockSpec((BLOCK, nk), prev_map),
            pl.BlockSpec((BLOCK, nk), cur_map),
            pl.BlockSpec((BLOCK, nk), next_map),
            pl.BlockSpec((BLOCK, 3 * BLOCK), lambda i: (0, 0)),
            smem,
            smem,
            pl.BlockSpec((BLOCK, d), cur_map),
            pl.BlockSpec((nq, d), lambda i: (0, 0)),
        ],
        out_specs=pl.BlockSpec((BLOCK, d), cur_map),
        out_shape=jax.ShapeDtypeStruct((t, d), F32),
        scratch_shapes=[
            pltpu.VMEM((N_HEADS, BLOCK, 3 * BLOCK), F32),
            pltpu.VMEM((BLOCK, nq), BF16),
        ],
        compiler_params=_params("arbitrary"),
        name="attn",
    )(q, k, k, k, v, v, v, bucket, rel_bias, sink, x2, w_o)


def _t5_bucket_table():
    half = N_BUCKETS // 2
    max_exact = half // 2
    qi = jnp.arange(BLOCK)[:, None]
    kj = jnp.arange(3 * BLOCK)[None, :]
    rel = kj - BLOCK - qi
    ret = jnp.where(rel > 0, half, 0)
    n = jnp.abs(rel)
    nf = jnp.maximum(n, 1).astype(jnp.float32)
    large = max_exact + (jnp.log(nf / max_exact) / math.log(MAX_DISTANCE / max_exact)
                         * (half - max_exact)).astype(jnp.int32)
    large = jnp.minimum(large, half - 1)
    return (ret + jnp.where(n < max_exact, n, large)).astype(jnp.int32)


INT32_MIN = -(2 ** 31)


def _ordered_int(bits):
    return bits ^ ((bits >> 31) & 0x7FFFFFFF)


def _top16_ranks(s, key_iota):
    bits = lax.bitcast_convert_type(s, jnp.int32)
    keys = _ordered_int(jnp.where(bits == INT32_MIN, 0, bits))
    slot = lax.broadcasted_iota(jnp.int32, (PEER_TOPK, s.shape[1]), 0)
    top = jnp.zeros((PEER_TOPK, s.shape[1]), jnp.int32)
    for r in range(PEER_TOPK):
        m = jnp.max(keys, axis=0, keepdims=True)
        first = jnp.min(jnp.where(keys == m, key_iota, float(N_KEYS)), axis=0, keepdims=True)
        keys = jnp.where(key_iota == first, INT32_MIN + r, keys)
        top = jnp.where(slot == r, m, top)
    rank = jnp.where(keys < INT32_MIN + PEER_TOPK, keys ^ INT32_MIN, PEER_TOPK).astype(F32)
    return rank, lax.bitcast_convert_type(_ordered_int(top), F32)


def _staircase(sv1, sv2):
    lanes = sv1.shape[1]
    slot = lax.broadcasted_iota(jnp.int32, (PEER_TOPK, lanes), 0).astype(F32)
    n = jnp.zeros((PEER_TOPK, lanes), F32)
    front = sv1[0:1, :] + sv2
    big = float(PEER_TOPK * PEER_TOPK + PEER_TOPK)
    for r in range(PEER_TOPK):
        m = jnp.max(front, axis=0, keepdims=True)
        flat = n * float(PEER_TOPK) + slot
        first = jnp.min(jnp.where(front == m, flat, big), axis=0, keepdims=True)
        chosen = flat == first
        n = jnp.where(chosen, n + 1.0, n)
        if r + 1 < PEER_TOPK:
            depth = jnp.sum(jnp.where(chosen, n, 0.0), axis=0, keepdims=True)
            nxt = jnp.sum(jnp.where(slot == depth, sv1, 0.0), axis=0, keepdims=True)
            front = jnp.where(chosen, jnp.where(depth < float(PEER_TOPK), nxt + sv2, -jnp.inf), front)
    return n


def _twice_bf16(v):
    bits = lax.bitcast_convert_type(v.astype(GATE_DTYPE).astype(F32), jnp.uint32)
    return lax.bitcast_convert_type(bits | (bits >> 16), jnp.int32)


def _peer_route_kernel(x_ref, g_ref, wq_ref, keys_ref, xn_ref, r1_ref, c_ref, n2_ref, w_ref, s_ref):
    tile = x_ref.shape[0]
    xn_t = _rmsnorm(x_ref[...], g_ref[...]).T.astype(BF16)
    xn_ref[...] = xn_t
    q_t = jnp.dot(wq_ref[...], xn_t, preferred_element_type=F32)
    for hc in range(PEER_HEADS * 2):
        q_hc = q_t[hc * SUB_DIM:(hc + 1) * SUB_DIM, :].astype(BF16)
        s_ref[hc] = jnp.dot(keys_ref[hc], q_hc, preferred_element_type=F32)

    key_iota = lax.broadcasted_iota(jnp.int32, (N_KEYS, LANES), 0).astype(F32)
    slot = lax.broadcasted_iota(jnp.int32, (PEER_TOPK, LANES), 0).astype(F32)

    def route_head(hd, lane_chunk):
        lane0 = pl.multiple_of(lane_chunk * LANES, LANES)
        lanes = pl.ds(lane0, LANES)
        s1 = s_ref[2 * hd, :, lanes]
        s2 = s_ref[2 * hd + 1, :, lanes]
        rank1, sv1 = _top16_ranks(s1, key_iota)
        rank2, sv2 = _top16_ranks(s2, key_iota)
        n = _staircase(sv1, sv2)
        e1 = jnp.exp(s1 - sv1[0:1, :])
        e2 = jnp.exp(s2 - sv2[0:1, :])
        e1s = jnp.exp(sv1 - sv1[0:1, :])
        e2s = jnp.exp(sv2 - sv2[0:1, :])
        inner = jnp.zeros_like(e2s)
        for p in range(PEER_TOPK):
            inner = inner + jnp.where(n > float(p), e1s[p:p + 1, :], 0.0)
        z = jnp.sum(inner * e2s, axis=0, keepdims=True)
        n2 = jnp.zeros_like(rank2)
        for q in range(PEER_TOPK):
            n2 = jnp.where(rank2 == float(q), n[q:q + 1, :], n2)
        r1_ref[hd, lane_chunk] = _twice_bf16(rank1)
        c_ref[hd, lane_chunk] = _twice_bf16(e1 / z)
        n2_ref[hd, :, lanes] = n2.astype(GATE_DTYPE)
        w_ref[hd, :, lanes] = e2.astype(GATE_DTYPE)

    def body(it, carry):
        lane_chunk = it % (tile // LANES)
        for i in range(ROUTE_HEADS_PER_ITER):
            route_head((it // (tile // LANES)) * ROUTE_HEADS_PER_ITER + i, lane_chunk)
        return carry

    lax.fori_loop(0, (PEER_HEADS // ROUTE_HEADS_PER_ITER) * (tile // LANES), body, 0)


def _peer_route(x2, g, wq_t, keys):
    t, d = x2.shape
    tile = min(ROUTE_TILE, t)
    nqd = PEER_HEADS * 2 * SUB_DIM
    scalar_shape = jax.ShapeDtypeStruct((PEER_HEADS, t // LANES, N_KEYS, LANES), jnp.int32)
    scalar_spec = pl.BlockSpec((PEER_HEADS, tile // LANES, N_KEYS, LANES), lambda i: (0, i, 0, 0))
    table_shape = jax.ShapeDtypeStruct((PEER_HEADS, N_KEYS, t), GATE_DTYPE)
    table_spec = pl.BlockSpec((PEER_HEADS, N_KEYS, tile), lambda i: (0, 0, i))
    return pl.pallas_call(
        _peer_route_kernel,
        grid=(t // tile,),
        in_specs=[
            pl.BlockSpec((tile, d), lambda i: (i, 0)),
            pl.BlockSpec((1, d), lambda i: (0, 0)),
            pl.BlockSpec((nqd, d), lambda i: (0, 0)),
            pl.BlockSpec((PEER_HEADS * 2, N_KEYS, SUB_DIM), lambda i: (0, 0, 0)),
        ],
        out_specs=[pl.BlockSpec((d, tile), lambda i: (0, i)), scalar_spec, scalar_spec, table_spec, table_spec],
        out_shape=[jax.ShapeDtypeStruct((d, t), BF16), scalar_shape, scalar_shape, table_shape, table_shape],
        scratch_shapes=[pltpu.VMEM((PEER_HEADS * 2, N_KEYS, tile), F32)],
        compiler_params=_params("parallel"),
        name="peer_route",
    )(x2, g, wq_t, keys)


def _plane_dot(lhs_planes, rows, rhs_ref, p, after=None):
    lhs = lhs_planes[p][rows, :]
    if after is not None:
        words = pltpu.bitcast(lhs, jnp.int32)
        reps = (words.shape[0] // after.shape[0], words.shape[1] // after.shape[1])
        lhs = pltpu.bitcast(words | jnp.tile(after, reps), lhs.dtype)
    return jnp.dot(lhs, rhs_ref[p * MXU_DEPTH:(p + 1) * MXU_DEPTH, :], preferred_element_type=F32)


def _k_planes_of_transpose(table):
    k, m = table.shape
    return jnp.swapaxes(table.astype(BF16).reshape(k // MXU_DEPTH, MXU_DEPTH, m), 1, 2)


def _zero_after(v):
    bits = lax.bitcast_convert_type(v, jnp.uint32)
    return lax.bitcast_convert_type((bits >> 16) >> 16, jnp.int32)


def _replicated_row(ref, lead, row):
    return ref[(*lead, pl.ds(row, SUBLANES, stride=0), slice(None))]


def _peer_dense_kernel(final_norm, n_chunks, n_planes, xn_ref, x_ref, *refs):
    u_planes = refs[:n_planes]
    vt_ref, r1_ref, c_ref, n2_ref, w_ref, fg_ref, o_ref, acc_ref, h0_ref, h1_ref, a0_ref, a1_ref = refs[n_planes:]
    vt_planes = [vt_ref.at[p] for p in range(vt_ref.shape[0])]
    s = pl.program_id(0)
    tile = xn_ref.shape[1]
    n_pieces = DENSE_PIECES
    up_rows = u_planes[0].shape[0] // n_pieces
    down_rows = vt_ref.shape[1] // n_pieces
    key_rows = N_KEYS // n_pieces
    rows_per_chunk = u_planes[0].shape[0] // N_KEYS
    slot = s % 2
    down_chunk = (s - 2) % n_chunks

    @pl.when(s == 0)
    def _():
        h1_ref[...] = jnp.zeros_like(h1_ref)
        a0_ref[...] = jnp.zeros_like(a0_ref)

    @pl.when((s < 2) | (down_chunk == 0))
    def _():
        acc_ref[...] = jnp.zeros_like(acc_ref)

    def run(h_up, h_gate, a_gate, a_down):
        def piece_body(k, carry):
            up = pl.ds(pl.multiple_of(k * up_rows, up_rows), up_rows)
            down = pl.ds(pl.multiple_of(k * down_rows, down_rows), down_rows)
            slabs = [(jg, sub) for jg in range(rows_per_chunk // SLAB_I1) for sub in range(key_rows // SLAB_ROWS)]
            slab = (SLAB_ROWS // BF16_ROWS, BF16_ROWS, tile)

            def scalar_rows(ref, hd, j, anchor):
                words = [_replicated_row(ref, (hd, lc), j) for lc in range(tile // LANES)]
                if anchor is not None:
                    words[-1] = words[-1] | anchor
                return jnp.concatenate([pltpu.bitcast(wd, GATE_DTYPE) for wd in words], axis=1)[None]

            def gate_slab(jg, sub, anchor):
                key0 = pl.multiple_of(k * key_rows + sub * SLAB_ROWS, SLAB_ROWS)
                keys = pl.ds(key0, SLAB_ROWS)
                i1_rows = range(jg * SLAB_I1, (jg + 1) * SLAB_I1)
                gates = [jnp.zeros(slab, GATE_DTYPE) for _ in i1_rows]
                for hd in range(PEER_HEADS):
                    n2 = n2_ref[hd, keys, :].reshape(slab)
                    w = w_ref[hd, keys, :].reshape(slab)
                    for idx, j in enumerate(i1_rows):
                        hold = anchor if (hd == PEER_HEADS - 1 and idx == SLAB_I1 - 1) else None
                        r1 = scalar_rows(r1_ref, hd, j, hold)
                        c = scalar_rows(c_ref, hd, j, None)
                        gates[idx] = gates[idx] + jnp.where(r1 < n2, w, jnp.zeros_like(w)) * c
                for idx, j in enumerate(i1_rows):
                    rows = pl.ds(pl.multiple_of(j * N_KEYS + key0, SLAB_ROWS), SLAB_ROWS)
                    h = h_gate[rows, :]
                    act = (0.5 * h * (1.0 + lax.erf(h * (1.0 / math.sqrt(2.0))))).astype(GATE_DTYPE)
                    out = (gates[idx].reshape(SLAB_ROWS, tile) * act).astype(a_gate.dtype)
                    a_gate[rows, :] = out
                return _zero_after(pltpu.bitcast(out[:, :LANES], jnp.int32))

            dots = ([("down", vt_planes, down, a_down, p) for p in range(len(vt_planes))]
                    + [("up", u_planes, up, xn_ref, p) for p in range(len(u_planes))])
            sums = {}
            tokens = []
            for i, (name, lhs_planes, rows, rhs_ref, p) in enumerate(dots):
                after = tokens[i - DOT_LAG] if i >= DOT_LAG else None
                part = _plane_dot(lhs_planes, rows, rhs_ref, p, after)
                sums[name] = part if p == 0 else sums[name] + part
                if (name, p) == ("up", len(u_planes) - 1):
                    h_up[up, :] = sums["up"]
                if (name, p) == ("down", len(vt_planes) - 1):
                    acc_ref[down, :] += sums["down"]
                for j, sub in slabs[i * len(slabs) // len(dots):(i + 1) * len(slabs) // len(dots)]:
                    anchor = None
                    if (j, sub) == slabs[-1]:
                        anchor = (_zero_after(sums["up"][-SUBLANES:, :LANES])
                                  | _zero_after(sums["up"][-SUBLANES:, -LANES:])
                                  | _zero_after(sums["down"][-SUBLANES:, :LANES])
                                  | _zero_after(sums["down"][-SUBLANES:, -LANES:]))
                    token = gate_slab(j, sub, anchor)
                tokens.append(token)
            return carry

        lax.fori_loop(0, n_pieces, piece_body, 0)

    @pl.when(slot == 0)
    def _():
        run(h0_ref, h1_ref, a1_ref, a0_ref)

    @pl.when(slot == 1)
    def _():
        run(h1_ref, h0_ref, a0_ref, a1_ref)

    @pl.when((s >= 2) & (down_chunk == n_chunks - 1))
    def _():
        y = x_ref[...] + acc_ref[...].T
        if final_norm:
            y = _rmsnorm(y, fg_ref[...])
        o_ref[...] = y


def _peer_dense(xn, x2, u, vt, r1, c, n2, w, final_g, final_norm):
    t, d = x2.shape
    n_experts = u.shape[0]
    n_planes = d // MXU_DEPTH
    tile = min(DENSE_TILE, t)
    ec = EXPERT_CHUNK
    rows = ec // N_KEYS
    n_chunks = n_experts // ec
    n_pos = (t // tile) * n_chunks

    def pos(s, lag):
        p = jnp.clip(s - lag, 0, n_pos - 1)
        return p // n_chunks, p % n_chunks

    up_tile = lambda s: pos(s, 0)[0]
    up_chunk = lambda s: pos(s, 0)[1]
    gate_tile = lambda s: pos(s, 1)[0]
    gate_chunk = lambda s: pos(s, 1)[1]
    down_tile = lambda s: pos(s, 2)[0]
    down_chunk = lambda s: pos(s, 2)[1]
    return pl.pallas_call(
        functools.partial(_peer_dense_kernel, final_norm, n_chunks, n_planes),
        grid=(n_pos + 2,),
        in_specs=[
            pl.BlockSpec((d, tile), lambda s: (0, up_tile(s))),
            pl.BlockSpec((tile, d), lambda s: (down_tile(s), 0)),
            *[pl.BlockSpec((ec, MXU_DEPTH), functools.partial(lambda p, s: (up_chunk(s), p), p))
              for p in range(n_planes)],
            pl.BlockSpec((ec // MXU_DEPTH, d, MXU_DEPTH), lambda s: (down_chunk(s), 0, 0)),
            pl.BlockSpec((PEER_HEADS, tile // LANES, rows, LANES), lambda s: (0, gate_tile(s), gate_chunk(s), 0)),
            pl.BlockSpec((PEER_HEADS, tile // LANES, rows, LANES), lambda s: (0, gate_tile(s), gate_chunk(s), 0)),
            pl.BlockSpec((PEER_HEADS, N_KEYS, tile), lambda s: (0, 0, gate_tile(s))),
            pl.BlockSpec((PEER_HEADS, N_KEYS, tile), lambda s: (0, 0, gate_tile(s))),
            pl.BlockSpec((1, d), lambda s: (0, 0)),
        ],
        out_specs=pl.BlockSpec((tile, d), lambda s: (down_tile(s), 0)),
        out_shape=jax.ShapeDtypeStruct((t, d), F32),
        scratch_shapes=[
            pltpu.VMEM((d, tile), F32),
            pltpu.VMEM((ec, tile), F32),
            pltpu.VMEM((ec, tile), F32),
            pltpu.VMEM((ec, tile), BF16),
            pltpu.VMEM((ec, tile), BF16),
        ],
        compiler_params=_params("arbitrary"),
        name="peer_dense",
    )(xn, x2, *([u] * n_planes), vt, r1, c, n2, w, final_g)


def _peer(x2, g, w_q, subkeys, u_tab, v_tab, final_g, final_norm):
    d = x2.shape[1]
    wq_t = w_q.astype(BF16).T
    keys = subkeys.reshape(PEER_HEADS * 2, N_KEYS, SUB_DIM).astype(BF16)
    xn, r1, c, n2, w = _peer_route(x2, g.reshape(1, d), wq_t, keys)
    return _peer_dense(xn, x2, u_tab.astype(BF16), _k_planes_of_transpose(v_tab), r1, c, n2, w,
                       final_g.reshape(1, d), final_norm)


def kernel(x, conv_norm_g, conv_w_in, conv_w, conv_w_out, attn_norm_g, attn_w_qkv, attn_sink, attn_w_o,
           rel_bias, ffn_norm_g, peer_w_q, peer_subkeys, peer_u, peer_v, final_norm_g):
    bsz, seq, d = x.shape
    depth = ffn_norm_g.shape[0]
    x2 = x.reshape(bsz * seq, d)
    bucket = _t5_bucket_table()
    for i in range(depth):
        j = i // 2
        if i % 2 == 0:
            u, gb = _conv_in(x2, conv_norm_g[j].reshape(1, d), conv_w_in[j].astype(BF16))
            x2 = _conv_out(u, gb, x2, conv_w[j], conv_w_out[j].astype(BF16), seq)
        else:
            q, k, v = _attn_qkv(x2, attn_norm_g[j].reshape(1, d), attn_w_qkv[j].astype(BF16))
            x2 = _attn(q, k, v, bucket, rel_bias, attn_sink[j], x2, attn_w_o[j].astype(BF16), seq)
        x2 = _peer(x2, ffn_norm_g[i], peer_w_q[i], peer_subkeys[i], peer_u[i], peer_v[i],
                   final_norm_g, i == depth - 1)
    return x2.reshape(bsz, seq, d)
```

```python
import functools
import math

import jax
import jax.numpy as jnp
import numpy as np
from jax import lax
from jax.experimental import pallas as pl
from jax.experimental.pallas import tpu as pltpu

F32 = jnp.float32
BF16 = jnp.bfloat16
GATE_DTYPE = jnp.bfloat16

RMS_EPS = 1e-6
CONV_WIDTH = 3
N_HEADS = 16
N_KV_HEADS = 4
HEAD_DIM = 64
GROUP = N_HEADS // N_KV_HEADS
WINDOW = 128
BLOCK = 128
NEG_INF = -1e30
N_BUCKETS = 32
MAX_DISTANCE = 128
PEER_HEADS = 8
N_KEYS = 128
PEER_TOPK = 16
SUB_DIM = 128

SUBLANES = 8
LANES = 128
BF16_ROWS = 2 * SUBLANES
MXU_DEPTH = 256
VMEM_LIMIT_BYTES = 52 * 1024 * 1024

CONV_TILE = 512
QKV_TILE = 512
ROUTE_TILE = 256
ROUTE_HEADS_PER_ITER = 8
DENSE_TILE = 512
EXPERT_CHUNK = 2048
DENSE_PIECES = 2
DOT_LAG = 1
SLAB_ROWS = 32
SLAB_I1 = 2


def _params(*semantics, flags=None):
    return pltpu.CompilerParams(dimension_semantics=semantics, vmem_limit_bytes=VMEM_LIMIT_BYTES,
                                flags=flags)


def _rmsnorm(x, g):
    ms = jnp.mean(x * x, axis=-1, keepdims=True)
    return x * lax.rsqrt(ms + RMS_EPS) * g


def _conv_in_kernel(x_ref, g_ref, w_ref, u_ref, gb_ref):
    d = x_ref.shape[1]
    xn = _rmsnorm(x_ref[...], g_ref[...]).astype(BF16)
    gb_ref[...] = jnp.dot(xn, w_ref[:, :d], preferred_element_type=F32)
    gate_c = jnp.dot(xn, w_ref[:, d:2 * d], preferred_element_type=F32)
    h = jnp.dot(xn, w_ref[:, 2 * d:], preferred_element_type=F32)
    u_ref[...] = gate_c * h


def _conv_in(x2, g, w_in):
    t, d = x2.shape
    tile = min(CONV_TILE, t)
    return pl.pallas_call(
        _conv_in_kernel,
        grid=(t // tile,),
        in_specs=[
            pl.BlockSpec((tile, d), lambda i: (i, 0)),
            pl.BlockSpec((1, d), lambda i: (0, 0)),
            pl.BlockSpec((d, 3 * d), lambda i: (0, 0)),
        ],
        out_specs=[
            pl.BlockSpec((tile, d), lambda i: (i, 0)),
            pl.BlockSpec((tile, d), lambda i: (i, 0)),
        ],
        out_shape=[jax.ShapeDtypeStruct((t, d), F32), jax.ShapeDtypeStruct((t, d), F32)],
        compiler_params=_params("parallel"),
        name="conv_in",
    )(x2, g, w_in)


def _conv_out_kernel(tiles_per_seq, u_ref, uprev_ref, unext_ref, gb_ref, x_ref, cw_ref, w_ref, o_ref):
    tile = u_ref.shape[0]
    pos = pl.program_id(0) % tiles_per_seq
    u = u_ref[...]
    prev_row = jnp.where(pos == 0, 0.0, uprev_ref[SUBLANES - 1:SUBLANES, :])
    next_row = jnp.where(pos == tiles_per_seq - 1, 0.0, unext_ref[0:1, :])
    row = lax.broadcasted_iota(jnp.int32, u.shape, 0)
    u_before = jnp.where(row == 0, prev_row, pltpu.roll(u, 1, axis=0))
    u_after = jnp.where(row == tile - 1, next_row, pltpu.roll(u, tile - 1, axis=0))
    y = cw_ref[0:1, :] * u_before + cw_ref[1:2, :] * u + cw_ref[2:3, :] * u_after
    z = (gb_ref[...] * y).astype(BF16)
    o_ref[...] = x_ref[...] + jnp.dot(z, w_ref[...], preferred_element_type=F32)


def _conv_out(u, gb, x2, conv_w, w_out, seq):
    t, d = x2.shape
    tile = min(CONV_TILE, seq)
    halo_per_tile = tile // SUBLANES
    n_halo = t // SUBLANES
    return pl.pallas_call(
        functools.partial(_conv_out_kernel, seq // tile),
        grid=(t // tile,),
        in_specs=[
            pl.BlockSpec((tile, d), lambda i: (i, 0)),
            pl.BlockSpec((SUBLANES, d), lambda i: (jnp.maximum(i * halo_per_tile - 1, 0), 0)),
            pl.BlockSpec((SUBLANES, d), lambda i: (jnp.minimum((i + 1) * halo_per_tile, n_halo - 1), 0)),
            pl.BlockSpec((tile, d), lambda i: (i, 0)),
            pl.BlockSpec((tile, d), lambda i: (i, 0)),
            pl.BlockSpec((CONV_WIDTH, d), lambda i: (0, 0)),
            pl.BlockSpec((d, d), lambda i: (0, 0)),
        ],
        out_specs=pl.BlockSpec((tile, d), lambda i: (i, 0)),
        out_shape=jax.ShapeDtypeStruct((t, d), F32),
        compiler_params=_params("parallel"),
        name="conv_out",
    )(u, u, u, gb, x2, conv_w, w_out)


def _attn_qkv_kernel(x_ref, g_ref, w_ref, q_ref, k_ref, v_ref):
    nq = q_ref.shape[1]
    nk = k_ref.shape[1]
    xn = _rmsnorm(x_ref[...], g_ref[...]).astype(BF16)
    q = jnp.dot(xn, w_ref[:, :nq], preferred_element_type=F32) * (1.0 / math.sqrt(HEAD_DIM))
    q_ref[...] = q.astype(BF16)
    k_ref[...] = jnp.dot(xn, w_ref[:, nq:nq + nk], preferred_element_type=F32).astype(BF16)
    v_ref[...] = jnp.dot(xn, w_ref[:, nq + nk:], preferred_element_type=F32).astype(BF16)


def _attn_qkv(x2, g, w_qkv):
    t, d = x2.shape
    tile = min(QKV_TILE, t)
    nq = N_HEADS * HEAD_DIM
    nk = N_KV_HEADS * HEAD_DIM
    return pl.pallas_call(
        _attn_qkv_kernel,
        grid=(t // tile,),
        in_specs=[
            pl.BlockSpec((tile, d), lambda i: (i, 0)),
            pl.BlockSpec((1, d), lambda i: (0, 0)),
            pl.BlockSpec((d, nq + 2 * nk), lambda i: (0, 0)),
        ],
        out_specs=[
            pl.BlockSpec((tile, nq), lambda i: (i, 0)),
            pl.BlockSpec((tile, nk), lambda i: (i, 0)),
            pl.BlockSpec((tile, nk), lambda i: (i, 0)),
        ],
        out_shape=[
            jax.ShapeDtypeStruct((t, nq), BF16),
            jax.ShapeDtypeStruct((t, nk), BF16),
            jax.ShapeDtypeStruct((t, nk), BF16),
        ],
        compiler_params=_params("parallel"),
        name="attn_qkv",
    )(x2, g, w_qkv)


def _attn_kernel(blocks_per_seq, q_ref, kp_ref, kc_ref, kn_ref, vp_ref, vc_ref, vn_ref, bucket_ref,
                 relb_ref, sink_ref, x_ref, wo_ref, o_ref, bias_ref, att_ref):
    step = pl.program_id(0)

    @pl.when(step == 0)
    def _():
        bucket = bucket_ref[...]
        rel = lax.broadcasted_iota(jnp.int32, bucket.shape, 1) - BLOCK - lax.broadcasted_iota(
            jnp.int32, bucket.shape, 0)
        in_window = jnp.abs(rel) <= WINDOW

        def head_body(hq, carry):
            def bucket_body(b, acc):
                return jnp.where(bucket == b, relb_ref[b, hq], acc)

            table = lax.fori_loop(0, N_BUCKETS, bucket_body, jnp.zeros(bucket.shape, F32))
            bias_ref[hq] = jnp.where(in_window, table, NEG_INF)
            return carry

        lax.fori_loop(0, N_HEADS, head_body, 0)

    pos = step % blocks_per_seq
    has_prev = pos > 0
    has_next = pos < blocks_per_seq - 1
    col = lax.broadcasted_iota(jnp.int32, (BLOCK, 3 * BLOCK), 1)
    key_ok = ((col >= BLOCK) | has_prev) & ((col < 2 * BLOCK) | has_next)

    for kvh in range(N_KV_HEADS):
        lo = kvh * HEAD_DIM
        k = jnp.concatenate([kp_ref[:, lo:lo + HEAD_DIM], kc_ref[:, lo:lo + HEAD_DIM],
                             kn_ref[:, lo:lo + HEAD_DIM]], axis=0)
        v = jnp.concatenate([vp_ref[:, lo:lo + HEAD_DIM], vc_ref[:, lo:lo + HEAD_DIM],
                             vn_ref[:, lo:lo + HEAD_DIM]], axis=0)
        for g in range(GROUP):
            hq = kvh * GROUP + g
            q = q_ref[:, hq * HEAD_DIM:(hq + 1) * HEAD_DIM]
            s = lax.dot_general(q, k, (((1,), (1,)), ((), ())), preferred_element_type=F32)
            logits = jnp.where(key_ok, s + bias_ref[hq], NEG_INF)
            sink = sink_ref[hq]
            m = jnp.maximum(jnp.max(logits, axis=-1, keepdims=True), sink)
            p = jnp.exp(logits - m)
            den = jnp.sum(p, axis=-1, keepdims=True) + jnp.exp(sink - m)
            o = jnp.dot(p.astype(BF16), v, preferred_element_type=F32) / den
            att_ref[:, hq * HEAD_DIM:(hq + 1) * HEAD_DIM] = o.astype(BF16)

    o_ref[...] = x_ref[...] + jnp.dot(att_ref[...], wo_ref[...], preferred_element_type=F32)


def _attn(q, k, v, bucket, rel_bias, sink, x2, w_o, seq):
    t, d = x2.shape
    nq = N_HEADS * HEAD_DIM
    nk = N_KV_HEADS * HEAD_DIM
    n_blocks = t // BLOCK
    prev_map = lambda i: (jnp.maximum(i - 1, 0), 0)
    cur_map = lambda i: (i, 0)
    next_map = lambda i: (jnp.minimum(i + 1, n_blocks - 1), 0)
    smem = pl.BlockSpec(memory_space=pltpu.SMEM)
    return pl.pallas_call(
        functools.partial(_attn_kernel, seq // BLOCK),
        grid=(n_blocks,),
        in_specs=[
            pl.BlockSpec((BLOCK, nq), cur_map),
            pl.BlockSpec((BLOCK, nk), prev_map),
            pl.BlockSpec((BLOCK, nk), cur_map),
            pl.BlockSpec((BLOCK, nk), next_map),
            pl.BlockSpec((BLOCK, nk), prev_map),
            pl.BlockSpec((BLOCK, nk), cur_map),
            pl.BlockSpec((BLOCK, nk), next_map),
            pl.BlockSpec((BLOCK, 3 * BLOCK), lambda i: (0, 0)),
            smem,
            smem,
            pl.BlockSpec((BLOCK, d), cur_map),
            pl.BlockSpec((nq, d), lambda i: (0, 0)),
        ],
        out_specs=pl.BlockSpec((BLOCK, d), cur_map),
        out_shape=jax.ShapeDtypeStruct((t, d), F32),
        scratch_shapes=[
            pltpu.VMEM((N_HEADS, BLOCK, 3 * BLOCK), F32),
            pltpu.VMEM((BLOCK, nq), BF16),
        ],
        compiler_params=_params("arbitrary"),
        name="attn",
    )(q, k, k, k, v, v, v, bucket, rel_bias, sink, x2, w_o)


def _t5_bucket_table():
    half = N_BUCKETS // 2
    max_exact = half // 2
    qi = jnp.arange(BLOCK)[:, None]
    kj = jnp.arange(3 * BLOCK)[None, :]
    rel = kj - BLOCK - qi
    ret = jnp.where(rel > 0, half, 0)
    n = jnp.abs(rel)
    nf = jnp.maximum(n, 1).astype(jnp.float32)
    large = max_exact + (jnp.log(nf / max_exact) / math.log(MAX_DISTANCE / max_exact)
                         * (half - max_exact)).astype(jnp.int32)
    large = jnp.minimum(large, half - 1)
    return (ret + jnp.where(n < max_exact, n, large)).astype(jnp.int32)


INT32_MIN = -(2 ** 31)


def _ordered_int(bits):
    return bits ^ ((bits >> 31) & 0x7FFFFFFF)


def _top16_ranks(s, key_iota):
    bits = lax.bitcast_convert_type(s, jnp.int32)
    keys = _ordered_int(jnp.where(bits == INT32_MIN, 0, bits))
    slot = lax.broadcasted_iota(jnp.int32, (PEER_TOPK, s.shape[1]), 0)
    top = jnp.zeros((PEER_TOPK, s.shape[1]), jnp.int32)
    for r in range(PEER_TOPK):
        m = jnp.max(keys, axis=0, keepdims=True)
        first = jnp.min(jnp.where(keys == m, key_iota, float(N_KEYS)), axis=0, keepdims=True)
        keys = jnp.where(key_iota == first, INT32_MIN + r, keys)
        top = jnp.where(slot == r, m, top)
    rank = jnp.where(keys < INT32_MIN + PEER_TOPK, keys ^ INT32_MIN, PEER_TOPK).astype(F32)
    return rank, lax.bitcast_convert_type(_ordered_int(top), F32)


def _staircase(sv1, sv2):
    lanes = sv1.shape[1]
    slot = lax.broadcasted_iota(jnp.int32, (PEER_TOPK, lanes), 0).astype(F32)
    n = jnp.zeros((PEER_TOPK, lanes), F32)
    front = sv1[0:1, :] + sv2
    big = float(PEER_TOPK * PEER_TOPK + PEER_TOPK)
    for r in range(PEER_TOPK):
        m = jnp.max(front, axis=0, keepdims=True)
        flat = n * float(PEER_TOPK) + slot
        first = jnp.min(jnp.where(front == m, flat, big), axis=0, keepdims=True)
        chosen = flat == first
        n = jnp.where(chosen, n + 1.0, n)
        if r + 1 < PEER_TOPK:
            depth = jnp.sum(jnp.where(chosen, n, 0.0), axis=0, keepdims=True)
            nxt = jnp.sum(jnp.where(slot == depth, sv1, 0.0), axis=0, keepdims=True)
            front = jnp.where(chosen, jnp.where(depth < float(PEER_TOPK), nxt + sv2, -jnp.inf), front)
    return n


def _twice_bf16(v):
    bits = lax.bitcast_convert_type(v.astype(GATE_DTYPE).astype(F32), jnp.uint32)
    return lax.bitcast_convert_type(bits | (bits >> 16), jnp.int32)


def _peer_route_kernel(x_ref, g_ref, wq_ref, keys_ref, xn_ref, r1_ref, c_ref, n2_ref, w_ref, s_ref):
    tile = x_ref.shape[0]
    xn_t = _rmsnorm(x_ref[...], g_ref[...]).T.astype(BF16)
    xn_ref[...] = xn_t
    q_t = jnp.dot(wq_ref[...], xn_t, preferred_element_type=F32)
    for hc in range(PEER_HEADS * 2):
        q_hc = q_t[hc * SUB_DIM:(hc + 1) * SUB_DIM, :].astype(BF16)
        s_ref[hc] = jnp.dot(keys_ref[hc], q_hc, preferred_element_type=F32)

    key_iota = lax.broadcasted_iota(jnp.int32, (N_KEYS, LANES), 0).astype(F32)
    slot = lax.broadcasted_iota(jnp.int32, (PEER_TOPK, LANES), 0).astype(F32)

    def route_head(hd, lane_chunk):
        lane0 = pl.multiple_of(lane_chunk * LANES, LANES)
        lanes = pl.ds(lane0, LANES)
        s1 = s_ref[2 * hd, :, lanes]
        s2 = s_ref[2 * hd + 1, :, lanes]
        rank1, sv1 = _top16_ranks(s1, key_iota)
        rank2, sv2 = _top16_ranks(s2, key_iota)
        n = _staircase(sv1, sv2)
        e1 = jnp.exp(s1 - sv1[0:1, :])
        e2 = jnp.exp(s2 - sv2[0:1, :])
        e1s = jnp.exp(sv1 - sv1[0:1, :])
        e2s = jnp.exp(sv2 - sv2[0:1, :])
        inner = jnp.zeros_like(e2s)
        for p in range(PEER_TOPK):
            inner = inner + jnp.where(n > float(p), e1s[p:p + 1, :], 0.0)
        z = jnp.sum(inner * e2s, axis=0, keepdims=True)
        n2 = jnp.zeros_like(rank2)
        for q in range(PEER_TOPK):
            n2 = jnp.where(rank2 == float(q), n[q:q + 1, :], n2)
        r1_ref[hd, lane_chunk] = _twice_bf16(rank1)
        c_ref[hd, lane_chunk] = _twice_bf16(e1 / z)
        n2_ref[hd, :, lanes] = n2.astype(GATE_DTYPE)
        w_ref[hd, :, lanes] = e2.astype(GATE_DTYPE)

    def body(it, carry):
        lane_chunk = it % (tile // LANES)
        for i in range(ROUTE_HEADS_PER_ITER):
            route_head((it // (tile // LANES)) * ROUTE_HEADS_PER_ITER + i, lane_chunk)
        return carry

    lax.fori_loop(0, (PEER_HEADS // ROUTE_HEADS_PER_ITER) * (tile // LANES), body, 0)


def _peer_route(x2, g, wq_t, keys):
    t, d = x2.shape
    tile = min(ROUTE_TILE, t)
    nqd = PEER_HEADS * 2 * SUB_DIM
    scalar_shape = jax.ShapeDtypeStruct((PEER_HEADS, t // LANES, N_KEYS, LANES), jnp.int32)
    scalar_spec = pl.BlockSpec((PEER_HEADS, tile // LANES, N_KEYS, LANES), lambda i: (0, i, 0, 0))
    table_shape = jax.ShapeDtypeStruct((PEER_HEADS, N_KEYS, t), GATE_DTYPE)
    table_spec = pl.BlockSpec((PEER_HEADS, N_KEYS, tile), lambda i: (0, 0, i))
    return pl.pallas_call(
        _peer_route_kernel,
        grid=(t // tile,),
        in_specs=[
            pl.BlockSpec((tile, d), lambda i: (i, 0)),
            pl.BlockSpec((1, d), lambda i: (0, 0)),
            pl.BlockSpec((nqd, d), lambda i: (0, 0)),
            pl.BlockSpec((PEER_HEADS * 2, N_KEYS, SUB_DIM), lambda i: (0, 0, 0)),
        ],
        out_specs=[pl.BlockSpec((d, tile), lambda i: (0, i)), scalar_spec, scalar_spec, table_spec, table_spec],
        out_shape=[jax.ShapeDtypeStruct((d, t), BF16), scalar_shape, scalar_shape, table_shape, table_shape],
        scratch_shapes=[pltpu.VMEM((PEER_HEADS * 2, N_KEYS, tile), F32)],
        compiler_params=_params("parallel"),
        name="peer_route",
    )(x2, g, wq_t, keys)


def _plane_dot(lhs_planes, rows, rhs_ref, p, after=None):
    lhs = lhs_planes[p][rows, :]
    if after is not None:
        words = pltpu.bitcast(lhs, jnp.int32)
        reps = (words.shape[0] // after.shape[0], words.shape[1] // after.shape[1])
        lhs = pltpu.bitcast(words | jnp.tile(after, reps), lhs.dtype)
    return jnp.dot(lhs, rhs_ref[p * MXU_DEPTH:(p + 1) * MXU_DEPTH, :], preferred_element_type=F32)


def _k_planes_of_transpose(table):
    k, m = table.shape
    return jnp.swapaxes(table.astype(BF16).reshape(k // MXU_DEPTH, MXU_DEPTH, m), 1, 2)


def _zero_after(v):
    bits = lax.bitcast_convert_type(v, jnp.uint32)
    return lax.bitcast_convert_type((bits >> 16) >> 16, jnp.int32)


def _replicated_row(ref, lead, row):
    return ref[(*lead, pl.ds(row, SUBLANES, stride=0), slice(None))]


def _peer_dense_kernel(final_norm, n_chunks, n_planes, xn_ref, x_ref, *refs):
    u_planes = refs[:n_planes]
    vt_ref, r1_ref, c_ref, n2_ref, w_ref, fg_ref, o_ref, acc_ref, h0_ref, h1_ref, a0_ref, a1_ref = refs[n_planes:]
    vt_planes = [vt_ref.at[p] for p in range(vt_ref.shape[0])]
    s = pl.program_id(0)
    tile = xn_ref.shape[1]
    n_pieces = DENSE_PIECES
    up_rows = u_planes[0].shape[0] // n_pieces
    down_rows = vt_ref.shape[1] // n_pieces
    key_rows = N_KEYS // n_pieces
    rows_per_chunk = u_planes[0].shape[0] // N_KEYS
    slot = s % 2
    down_chunk = (s - 2) % n_chunks

    @pl.when(s == 0)
    def _():
        h1_ref[...] = jnp.zeros_like(h1_ref)
        a0_ref[...] = jnp.zeros_like(a0_ref)

    @pl.when((s < 2) | (down_chunk == 0))
    def _():
        acc_ref[...] = jnp.zeros_like(acc_ref)

    def run(h_up, h_gate, a_gate, a_down):
        def piece_body(k, carry):
            up = pl.ds(pl.multiple_of(k * up_rows, up_rows), up_rows)
            down = pl.ds(pl.multiple_of(k * down_rows, down_rows), down_rows)
            slabs = [(jg, sub) for jg in range(rows_per_chunk // SLAB_I1) for sub in range(key_rows // SLAB_ROWS)]
            slab = (SLAB_ROWS // BF16_ROWS, BF16_ROWS, tile)

            def scalar_rows(ref, hd, j, anchor):
                words = [_replicated_row(ref, (hd, lc), j) for lc in range(tile // LANES)]
                if anchor is not None:
                    words[-1] = words[-1] | anchor
                return jnp.concatenate([pltpu.bitcast(wd, GATE_DTYPE) for wd in words], axis=1)[None]

            def gate_slab(jg, sub, anchor):
                key0 = pl.multiple_of(k * key_rows + sub * SLAB_ROWS, SLAB_ROWS)
                keys = pl.ds(key0, SLAB_ROWS)
                i1_rows = range(jg * SLAB_I1, (jg + 1) * SLAB_I1)
                gates = [jnp.zeros(slab, GATE_DTYPE) for _ in i1_rows]
                for hd in range(PEER_HEADS):
                    n2 = n2_ref[hd, keys, :].reshape(slab)
                    w = w_ref[hd, keys, :].reshape(slab)
                    for idx, j in enumerate(i1_rows):
                        hold = anchor if (hd == PEER_HEADS - 1 and idx == SLAB_I1 - 1) else None
                        r1 = scalar_rows(r1_ref, hd, j, hold)
                        c = scalar_rows(c_ref, hd, j, None)
                        gates[idx] = gates[idx] + jnp.where(r1 < n2, w, jnp.zeros_like(w)) * c
                for idx, j in enumerate(i1_rows):
                    rows = pl.ds(pl.multiple_of(j * N_KEYS + key0, SLAB_ROWS), SLAB_ROWS)
                    h = h_gate[rows, :]
                    act = (0.5 * h * (1.0 + lax.erf(h * (1.0 / math.sqrt(2.0))))).astype(GATE_DTYPE)
                    out = (gates[idx].reshape(SLAB_ROWS, tile) * act).astype(a_gate.dtype)
                    a_gate[rows, :] = out
                return _zero_after(pltpu.bitcast(out[:, :LANES], jnp.int32))

            dots = ([("up", u_planes, up, xn_ref, p) for p in range(len(u_planes))]
                    + [("down", vt_planes, down, a_down, p) for p in range(len(vt_planes))])
            sums = {}
            tokens = []
            for i, (name, lhs_planes, rows, rhs_ref, p) in enumerate(dots):
                after = tokens[i - DOT_LAG] if i >= DOT_LAG else None
                part = _plane_dot(lhs_planes, rows, rhs_ref, p, after)
                sums[name] = part if p == 0 else sums[name] + part
                if (name, p) == ("up", len(u_planes) - 1):
                    h_up[up, :] = sums["up"]
                if (name, p) == ("down", len(vt_planes) - 1):
                    acc_ref[down, :] += sums["down"]
                for j, sub in slabs[i * len(slabs) // len(dots):(i + 1) * len(slabs) // len(dots)]:
                    anchor = None
                    if (j, sub) == slabs[-1]:
                        anchor = (_zero_after(sums["up"][-SUBLANES:, :LANES])
                                  | _zero_after(sums["up"][-SUBLANES:, -LANES:])
                                  | _zero_after(sums["down"][-SUBLANES:, :LANES])
                                  | _zero_after(sums["down"][-SUBLANES:, -LANES:]))
                    token = gate_slab(j, sub, anchor)
                tokens.append(token)
            return carry

        lax.fori_loop(0, n_pieces, piece_body, 0)

    @pl.when(slot == 0)
    def _():
        run(h0_ref, h1_ref, a1_ref, a0_ref)

    @pl.when(slot == 1)
    def _():
        run(h1_ref, h0_ref, a0_ref, a1_ref)

    @pl.when((s >= 2) & (down_chunk == n_chunks - 1))
    def _():
        y = x_ref[...] + acc_ref[...].T
        if final_norm:
            y = _rmsnorm(y, fg_ref[...])
        o_ref[...] = y


def _peer_dense(xn, x2, u, vt, r1, c, n2, w, final_g, final_norm):
    t, d = x2.shape
    n_experts = u.shape[0]
    n_planes = d // MXU_DEPTH
    tile = min(DENSE_TILE, t)
    ec = EXPERT_CHUNK
    rows = ec // N_KEYS
    n_chunks = n_experts // ec
    n_pos = (t // tile) * n_chunks

    def pos(s, lag):
        p = jnp.clip(s - lag, 0, n_pos - 1)
        return p // n_chunks, p % n_chunks

    up_tile = lambda s: pos(s, 0)[0]
    up_chunk = lambda s: pos(s, 0)[1]
    gate_tile = lambda s: pos(s, 1)[0]
    gate_chunk = lambda s: pos(s, 1)[1]
    down_tile = lambda s: pos(s, 2)[0]
    down_chunk = lambda s: pos(s, 2)[1]
    return pl.pallas_call(
        functools.partial(_peer_dense_kernel, final_norm, n_chunks, n_planes),
        grid=(n_pos + 2,),
        in_specs=[
            pl.BlockSpec((d, tile), lambda s: (0, up_tile(s))),
            pl.BlockSpec((tile, d), lambda s: (down_tile(s), 0)),
            *[pl.BlockSpec((ec, MXU_DEPTH), functools.partial(lambda p, s: (up_chunk(s), p), p))
              for p in range(n_planes)],
            pl.BlockSpec((ec // MXU_DEPTH, d, MXU_DEPTH), lambda s: (down_chunk(s), 0, 0)),
            pl.BlockSpec((PEER_HEADS, tile // LANES, rows, LANES), lambda s: (0, gate_tile(s), gate_chunk(s), 0)),
            pl.BlockSpec((PEER_HEADS, tile // LANES, rows, LANES), lambda s: (0, gate_tile(s), gate_chunk(s), 0)),
            pl.BlockSpec((PEER_HEADS, N_KEYS, tile), lambda s: (0, 0, gate_tile(s))),
            pl.BlockSpec((PEER_HEADS, N_KEYS, tile), lambda s: (0, 0, gate_tile(s))),
            pl.BlockSpec((1, d), lambda s: (0, 0)),
        ],
        out_specs=pl.BlockSpec((tile, d), lambda s: (down_tile(s), 0)),
        out_shape=jax.ShapeDtypeStruct((t, d), F32),
        scratch_shapes=[
            pltpu.VMEM((d, tile), F32),
            pltpu.VMEM((ec, tile), F32),
            pltpu.VMEM((ec, tile), F32),
            pltpu.VMEM((ec, tile), BF16),
            pltpu.VMEM((ec, tile), BF16),
        ],
        compiler_params=_params("arbitrary"),
        name="peer_dense",
    )(xn, x2, *([u] * n_planes), vt, r1, c, n2, w, final_g)


def _peer(x2, g, w_q, subkeys, u_tab, v_tab, final_g, final_norm):
    d = x2.shape[1]
    wq_t = w_q.astype(BF16).T
    keys = subkeys.reshape(PEER_HEADS * 2, N_KEYS, SUB_DIM).astype(BF16)
    xn, r1, c, n2, w = _peer_route(x2, g.reshape(1, d), wq_t, keys)
    return _peer_dense(xn, x2, u_tab.astype(BF16), _k_planes_of_transpose(v_tab), r1, c, n2, w,
                       final_g.reshape(1, d), final_norm)


def kernel(x, conv_norm_g, conv_w_in, conv_w, conv_w_out, attn_norm_g, attn_w_qkv, attn_sink, attn_w_o,
           rel_bias, ffn_norm_g, peer_w_q, peer_subkeys, peer_u, peer_v, final_norm_g):
    bsz, seq, d = x.shape
    depth = ffn_norm_g.shape[0]
    x2 = x.reshape(bsz * seq, d)
    bucket = _t5_bucket_table()
    for i in range(depth):
        j = i // 2
        if i % 2 == 0:
            u, gb = _conv_in(x2, conv_norm_g[j].reshape(1, d), conv_w_in[j].astype(BF16))
            x2 = _conv_out(u, gb, x2, conv_w[j], conv_w_out[j].astype(BF16), seq)
        else:
            q, k, v = _attn_qkv(x2, attn_norm_g[j].reshape(1, d), attn_w_qkv[j].astype(BF16))
            x2 = _attn(q, k, v, bucket, rel_bias, attn_sink[j], x2, attn_w_o[j].astype(BF16), seq)
        x2 = _peer(x2, ffn_norm_g[i], peer_w_q[i], peer_subkeys[i], peer_u[i], peer_v[i],
                   final_norm_g, i == depth - 1)
    return x2.reshape(bsz, seq, d)
```
